```python
import math
import jax
import jax.numpy as jnp
from jax import lax
import numpy as np

D_MODEL = 2048
BATCH = 8
SEQ = 2048
DEPTH = 2

GRID_W = 64
CTX_LEN = 256
N_MIXERS = 2
N_LAYERS_A = (DEPTH + N_MIXERS - 1) // N_MIXERS
N_LAYERS_B = DEPTH // N_MIXERS
NORM_EPS = 1e-6
N_MOD = 6

GMLP_CHUNK = 128
GMLP_WIDTH = D_MODEL
GMLP_GROUPS = 8

DIFF_HEADS = 8
DIFF_HEAD_DIM = D_MODEL // DIFF_HEADS // 2
QUERY_BLOCK = 128
ROPE_THETA = 10000.0

N_EXPERTS = 32
TOP_K = 4
D_EXPERT = D_MODEL
SWIGLU_ALPHA = 1.702
SWIGLU_LIMIT = 7.0
EXPERT_BLOCK = 128

kernel_name = "hybrid_gmlp_diffattn_moe_dit"


def rms_norm(x, g):
    xf = x.astype(jnp.float32)
    y = xf * lax.rsqrt(jnp.mean(xf * xf, axis=-1, keepdims=True) + NORM_EPS)
    return (y * g.astype(jnp.float32)).astype(x.dtype)


def layer_norm(x, g):
    xf = x.astype(jnp.float32)
    xc = xf - jnp.mean(xf, axis=-1, keepdims=True)
    y = xc * lax.rsqrt(jnp.mean(xc * xc, axis=-1, keepdims=True) + NORM_EPS)
    return (y * g.astype(jnp.float32)).astype(x.dtype)


def modulate(h, shift, scale):
    return h * (1 + scale) + shift


def axial_rope_tables(rows, dtype):
    row_pos = jnp.repeat(jnp.arange(rows), GRID_W).astype(jnp.float32)
    col_pos = jnp.tile(jnp.arange(GRID_W), rows).astype(jnp.float32)
    n_freq = DIFF_HEAD_DIM // 4
    inv_freq = ROPE_THETA ** (-jnp.arange(n_freq, dtype=jnp.float32) / n_freq)
    ang = jnp.stack([row_pos[:, None] * inv_freq, col_pos[:, None] * inv_freq], axis=1)
    return jnp.cos(ang)[:, None].astype(dtype), jnp.sin(ang)[:, None].astype(dtype)


def apply_axial_rope(x, cos, sin):
    b, l, h, dh = x.shape
    xr = x.reshape(b, l, h, 2, 2, dh // 4)
    x1, x2 = xr[..., 0, :], xr[..., 1, :]
    out = jnp.stack([x1 * cos - x2 * sin, x2 * cos + x1 * sin], axis=-2)
    return out.reshape(b, l, h, dh)


def chunk_gmlp(h, w_in, norm_g, w_s, b_s, w_out):
    b, l, _ = h.shape
    u, v = jnp.split(jax.nn.gelu(h @ w_in, approximate=False), 2, axis=-1)
    v = layer_norm(v, norm_g).reshape(b, l // GMLP_CHUNK, GMLP_CHUNK, GMLP_GROUPS, GMLP_WIDTH // GMLP_GROUPS)
    mixed = jnp.einsum('gpq,bnqgc->bnpgc', w_s, v) + b_s.T[:, :, None]
    return (u * mixed.reshape(b, l, GMLP_WIDTH)) @ w_out


def diff_attn_block(q, k, v, lam):
    s = jnp.einsum('bqhd,bkhd->bhqk', q, k).astype(jnp.float32) * (DIFF_HEAD_DIM ** -0.5)
    p = jax.nn.softmax(s, axis=-1)
    b, _, nq, nk = p.shape
    p = p.reshape(b, DIFF_HEADS, 2, nq, nk)
    a = p[:, :, 0] - lam * p[:, :, 1]
    return jnp.einsum('bhqk,bkhe->bqhe', a.astype(v.dtype), v)


def diff_attention(h_lat, h_ctx, cos, sin, layer_idx, need_ctx_out,
                   w_qkv, q_norm_g, k_norm_g, lam_vecs, subln_g, w_o):
    b, s_len, _ = h_lat.shape

    def project(h):
        l = h.shape[1]
        q, k, v = jnp.split(h @ w_qkv, 3, axis=-1)
        q = rms_norm(q.reshape(b, l, 2 * DIFF_HEADS, DIFF_HEAD_DIM), q_norm_g)
        k = rms_norm(k.reshape(b, l, 2 * DIFF_HEADS, DIFF_HEAD_DIM), k_norm_g)
        v = v.reshape(b, l, DIFF_HEADS, 2 * DIFF_HEAD_DIM)
        return q, k, v

    q_lat, k_lat, v_lat = project(h_lat)
    q_ctx, k_ctx, v_ctx = project(h_ctx)
    q_lat = apply_axial_rope(q_lat, cos, sin)
    k_lat = apply_axial_rope(k_lat, cos, sin)

    lam_init = 0.8 - 0.6 * math.exp(-0.3 * layer_idx)
    lf = lam_vecs.astype(jnp.float32)
    lam = jnp.exp(jnp.sum(lf[0] * lf[1])) - jnp.exp(jnp.sum(lf[2] * lf[3])) + lam_init

    k_all = jnp.concatenate([k_ctx, k_lat], axis=1)
    v_all = jnp.concatenate([v_ctx, v_lat], axis=1)
    n_blk = s_len // QUERY_BLOCK
    q_blocks = q_lat.reshape(b, n_blk, QUERY_BLOCK, 2 * DIFF_HEADS, DIFF_HEAD_DIM).transpose(1, 0, 2, 3, 4)
    o = lax.map(lambda qb: diff_attn_block(qb, k_all, v_all, lam), q_blocks)
    o_lat = o.transpose(1, 0, 2, 3, 4).reshape(b, s_len, DIFF_HEADS, 2 * DIFF_HEAD_DIM)

    def finish(o_heads):
        o_heads = rms_norm(o_heads, subln_g) * (1 - lam_init)
        return o_heads.reshape(b, o_heads.shape[1], D_MODEL) @ w_o

    y_lat = finish(o_lat)
    y_ctx = finish(diff_attn_block(q_ctx, k_ctx, v_ctx, lam)) if need_ctx_out else None
    return y_lat, y_ctx


def moe_ffn(h, w_router, b_router, w_gu, b_gu, w_down, b_down):
    n_tok, d = h.shape
    logits = (h @ w_router + b_router).astype(jnp.float32)
    top_logit, top_idx = lax.top_k(logits, TOP_K)
    top_w = jax.nn.softmax(top_logit, axis=-1).astype(h.dtype)

    n_slots = n_tok * TOP_K
    slot_e = top_idx.reshape(-1)
    slot_tok = jnp.repeat(jnp.arange(n_tok, dtype=jnp.int32), TOP_K)
    slot_w = top_w.reshape(-1)
    order = jnp.argsort(slot_e)
    e_sorted, tok_sorted, w_sorted = slot_e[order], slot_tok[order], slot_w[order]

    counts = jnp.bincount(slot_e, length=N_EXPERTS)
    starts = jnp.cumsum(counts) - counts
    padded = (counts + EXPERT_BLOCK - 1) // EXPERT_BLOCK * EXPERT_BLOCK
    pad_ends = jnp.cumsum(padded)
    pad_starts = pad_ends - padded
    dest = pad_starts[e_sorted] + jnp.arange(n_slots, dtype=jnp.int32) - starts[e_sorted]

    n_blocks = -(-(n_slots + N_EXPERTS * (EXPERT_BLOCK - 1)) // EXPERT_BLOCK)
    n_pad = n_blocks * EXPERT_BLOCK
    buf_tok = jnp.full((n_pad,), n_tok, jnp.int32).at[dest].set(tok_sorted)
    buf_w = jnp.zeros((n_pad,), h.dtype).at[dest].set(w_sorted)
    blk_e = jnp.minimum(jnp.searchsorted(pad_ends, jnp.arange(n_blocks) * EXPERT_BLOCK, side='right'),
                        N_EXPERTS - 1)
    h_pad = jnp.concatenate([h, jnp.zeros((1, d), h.dtype)], axis=0)

    def expert_block(args):
        tok, e = args
        gu = h_pad[tok] @ w_gu[e] + b_gu[e]
        glu = jnp.minimum(gu[:, 0::2], SWIGLU_LIMIT)
        lin = jnp.clip(gu[:, 1::2], -SWIGLU_LIMIT, SWIGLU_LIMIT)
        act = glu * jax.nn.sigmoid(SWIGLU_ALPHA * glu) * (lin + 1)
        return act @ w_down[e] + b_down[e]

    y = lax.map(expert_block, (buf_tok.reshape(n_blocks, EXPERT_BLOCK), blk_e))
    y = y.reshape(n_pad, d) * buf_w[:, None]
    return jax.ops.segment_sum(y, buf_tok, num_segments=n_tok + 1)[:n_tok]


def setup_inputs(seed: int = 0) -> dict:
    key = jax.random.key(seed)
    ks = jax.random.split(key, 26)
    f32 = jnp.float32
    D = D_MODEL
    HD = DIFF_HEAD_DIM

    def nrm(k, shape, s):
        return jax.random.normal(k, shape, f32) * s

    return {
        "x": nrm(ks[0], (BATCH, SEQ, D), 1.0),
        "c": nrm(ks[1], (BATCH, D), 1.0),
        "ctx": nrm(ks[2], (BATCH, CTX_LEN, D), 1.0),
        "c_ctx": nrm(ks[3], (D,), 1.0),
        "w_mod": nrm(ks[4], (DEPTH, D, N_MOD * D), 0.5 * D ** -0.5),
        "b_mod": nrm(ks[5], (DEPTH, N_MOD * D), 0.01),
        "norm_mix_g": 1.0 + nrm(ks[6], (DEPTH, D), 0.05),
        "norm_ffn_g": 1.0 + nrm(ks[7], (DEPTH, D), 0.05),
        "gmlp_w_in": nrm(ks[8], (N_LAYERS_A, D, 2 * GMLP_WIDTH), D ** -0.5),
        "gmlp_norm_g": 1.0 + nrm(ks[9], (N_LAYERS_A, GMLP_WIDTH), 0.05),
        "gmlp_w_s": nrm(ks[10], (N_LAYERS_A, GMLP_GROUPS, GMLP_CHUNK, GMLP_CHUNK), GMLP_CHUNK ** -0.5),
        "gmlp_b_s": 1.0 + nrm(ks[11], (N_LAYERS_A, GMLP_GROUPS, GMLP_CHUNK), 0.05),
        "gmlp_w_out": nrm(ks[12], (N_LAYERS_A, GMLP_WIDTH, D), GMLP_WIDTH ** -0.5),
        "diff_w_qkv": nrm(ks[13], (N_LAYERS_B, D, 3 * D), D ** -0.5),
        "diff_q_norm_g": 1.0 + nrm(ks[14], (N_LAYERS_B, HD), 0.05),
        "diff_k_norm_g": 1.0 + nrm(ks[15], (N_LAYERS_B, HD), 0.05),
        "diff_lambda": nrm(ks[16], (N_LAYERS_B, 4, HD), 0.1),
        "diff_subln_g": 1.0 + nrm(ks[17], (N_LAYERS_B, 2 * HD), 0.05),
        "diff_w_o": nrm(ks[18], (N_LAYERS_B, D, D), D ** -0.5),
        "moe_w_router": nrm(ks[19], (DEPTH, D, N_EXPERTS), D ** -0.5),
        "moe_b_router": nrm(ks[20], (DEPTH, N_EXPERTS), 0.01),
        "moe_w_gate_up": nrm(ks[21], (DEPTH, N_EXPERTS, D, 2 * D_EXPERT), D ** -0.5),
        "moe_b_gate_up": nrm(ks[22], (DEPTH, N_EXPERTS, 2 * D_EXPERT), 0.01),
        "moe_w_down": nrm(ks[23], (DEPTH, N_EXPERTS, D_EXPERT, D), D_EXPERT ** -0.5),
        "moe_b_down": nrm(ks[24], (DEPTH, N_EXPERTS, D), 0.01),
    }


def reference(x, c, ctx, c_ctx, w_mod, b_mod, norm_mix_g, norm_ffn_g,
              gmlp_w_in, gmlp_norm_g, gmlp_w_s, gmlp_b_s, gmlp_w_out,
              diff_w_qkv, diff_q_norm_g, diff_k_norm_g, diff_lambda, diff_subln_g, diff_w_o,
              moe_w_router, moe_b_router, moe_w_gate_up, moe_b_gate_up, moe_w_down, moe_b_down):
    b, s_len, d = x.shape
    rows = s_len // GRID_W
    cos, sin = axial_rope_tables(rows, x.dtype)
    silu_c = jax.nn.silu(c)
    silu_cc = jax.nn.silu(c_ctx)
    x_lat, x_ctx = x, ctx

    for i in range(DEPTH):
        need_ctx = i < DEPTH - 1
        j = i // N_MIXERS
        mod_lat = (silu_c @ w_mod[i] + b_mod[i]).reshape(b, 1, N_MOD, d)
        mod_ctx = (silu_cc @ w_mod[i] + b_mod[i]).reshape(1, 1, N_MOD, d)

        h_lat = modulate(rms_norm(x_lat, norm_mix_g[i]), mod_lat[:, :, 0], mod_lat[:, :, 1])
        h_ctx = modulate(rms_norm(x_ctx, norm_mix_g[i]), mod_ctx[:, :, 0], mod_ctx[:, :, 1])
        if i % N_MIXERS == 0:
            gp = (gmlp_w_in[j], gmlp_norm_g[j], gmlp_w_s[j], gmlp_b_s[j], gmlp_w_out[j])
            y_lat = chunk_gmlp(h_lat, *gp)
            y_ctx = chunk_gmlp(h_ctx, *gp) if need_ctx else None
        else:
            y_lat, y_ctx = diff_attention(h_lat, h_ctx, cos, sin, i, need_ctx,
                                          diff_w_qkv[j], diff_q_norm_g[j], diff_k_norm_g[j],
                                          diff_lambda[j], diff_subln_g[j], diff_w_o[j])
        x_lat = x_lat + mod_lat[:, :, 2] * y_lat

        mp = (moe_w_router[i], moe_b_router[i], moe_w_gate_up[i], moe_b_gate_up[i],
              moe_w_down[i], moe_b_down[i])
        h_lat = modulate(rms_norm(x_lat, norm_ffn_g[i]), mod_lat[:, :, 3], mod_lat[:, :, 4])
        if need_ctx:
            x_ctx = x_ctx + mod_ctx[:, :, 2] * y_ctx
            h_ctx = modulate(rms_norm(x_ctx, norm_ffn_g[i]), mod_ctx[:, :, 3], mod_ctx[:, :, 4])
            n_lat = b * s_len
            y_all = moe_ffn(jnp.concatenate([h_lat.reshape(-1, d), h_ctx.reshape(-1, d)], axis=0), *mp)
            x_ctx = x_ctx + mod_ctx[:, :, 5] * y_all[n_lat:].reshape(b, -1, d)
            y_lat = y_all[:n_lat].reshape(b, s_len, d)
        else:
            y_lat = moe_ffn(h_lat.reshape(-1, d), *mp).reshape(b, s_len, d)
        x_lat = x_lat + mod_lat[:, :, 5] * y_lat

    return x_lat
```

```python
import functools
import math

import jax
import jax.numpy as jnp
from jax import lax
from jax.experimental import pallas as pl
from jax.experimental.pallas import tpu as pltpu

F32 = jnp.float32
BF16 = jnp.bfloat16

NORM_EPS = 1e-6
N_MOD = 6
GRID_W = 64
GMLP_CHUNK = 128
GMLP_GROUPS = 8
DIFF_HEADS = 8
HEAD_DIM = 128
ROPE_THETA = 10000.0
N_EXPERTS = 32
TOP_K = 4
SWIGLU_ALPHA = 1.702
SWIGLU_LIMIT = 7.0

MOD_ROWS = 16
EXPERT_ROWS = 256
MIB = 1024 * 1024


def _cparams(vmem_mib):
    return pltpu.CompilerParams(vmem_limit_bytes=vmem_mib * MIB)


def _rms_mod(x, g, shift, scale):
    ms = jnp.mean(x * x, axis=-1, keepdims=True)
    y = x * lax.rsqrt(ms + NORM_EPS) * g
    return y * (1.0 + scale) + shift


def _mod_kernel(c_ref, w_ref, b_ref, o_ref):
    c = c_ref[...]
    s = c * jax.nn.sigmoid(c)
    o_ref[0] = jnp.dot(s.astype(BF16), w_ref[0].astype(BF16), preferred_element_type=F32) + b_ref[0]


def _modulation(c16, w_mod, b_mod):
    depth, d, n = w_mod.shape
    tn = 1024
    return pl.pallas_call(
        _mod_kernel,
        grid=(depth, n // tn),
        in_specs=[
            pl.BlockSpec((MOD_ROWS, d), lambda l, j: (0, 0)),
            pl.BlockSpec((1, d, tn), lambda l, j: (l, 0, j)),
            pl.BlockSpec((1, 1, tn), lambda l, j: (l, 0, j)),
        ],
        out_specs=pl.BlockSpec((1, MOD_ROWS, tn), lambda l, j: (l, 0, j)),
        out_shape=jax.ShapeDtypeStruct((depth, MOD_ROWS, n), F32),
        compiler_params=_cparams(40),
        name="modulation",
    )(c16, w_mod, b_mod.reshape(depth, 1, n))


def _gmlp_in_kernel(x_ref, g_ref, sh_ref, sc_ref, w_ref, o_ref, h_ref):
    @pl.when(pl.program_id(1) == 0)
    def _():
        h_ref[...] = _rms_mod(x_ref[...], g_ref[...], sh_ref[0], sc_ref[0]).astype(BF16)

    acc = jnp.dot(h_ref[...], w_ref[...], preferred_element_type=F32)
    gelu = 0.5 * acc * (1.0 + lax.erf(acc * (2.0 ** -0.5)))
    o_ref[...] = gelu.astype(o_ref.dtype)


def _gmlp_in(x, g, shift, scale, w_bf16, ridx, tm=1024, tn=1024):
    t, d = x.shape
    n = w_bf16.shape[1]
    return pl.pallas_call(
        _gmlp_in_kernel,
        grid=(t // tm, n // tn),
        in_specs=[
            pl.BlockSpec((tm, d), lambda i, j: (i, 0)),
            pl.BlockSpec((1, d), lambda i, j: (0, 0)),
            pl.BlockSpec((1, 1, d), lambda i, j: (ridx(i, tm), 0, 0)),
            pl.BlockSpec((1, 1, d), lambda i, j: (ridx(i, tm), 0, 0)),
            pl.BlockSpec((d, tn), lambda i, j: (0, j)),
        ],
        out_specs=pl.BlockSpec((tm, tn), lambda i, j: (i, j)),
        out_shape=jax.ShapeDtypeStruct((t, n), BF16),
        scratch_shapes=[pltpu.VMEM((tm, d), BF16)],
        compiler_params=_cparams(48),
        name="gmlp_in",
    )(x, g, shift, scale, w_bf16)


def _gmlp_out_kernel(u_ref, v_ref, lng_ref, ws_ref, bs_ref, wo_ref, x_ref, gate_ref, o_ref, t_ref):
    tm, width = v_ref.shape
    gw = width // GMLP_GROUPS
    v = v_ref[...].astype(F32)
    mu = jnp.mean(v, axis=-1, keepdims=True)
    vc = v - mu
    var = jnp.mean(vc * vc, axis=-1, keepdims=True)
    vn = (vc * lax.rsqrt(var + NORM_EPS) * lng_ref[...]).astype(BF16)
    for c in range(tm // GMLP_CHUNK):
        r0 = c * GMLP_CHUNK
        for g in range(GMLP_GROUPS):
            c0 = g * gw
            mixed = jnp.dot(ws_ref[g], vn[r0:r0 + GMLP_CHUNK, c0:c0 + gw], preferred_element_type=F32)
            mixed = mixed + bs_ref[:, g:g + 1]
            u = u_ref[r0:r0 + GMLP_CHUNK, c0:c0 + gw].astype(F32)
            t_ref[r0:r0 + GMLP_CHUNK, c0:c0 + gw] = (u * mixed).astype(BF16)
    y = jnp.dot(t_ref[...], wo_ref[...], preferred_element_type=F32)
    o_ref[...] = x_ref[...] + gate_ref[0] * y


def _gmlp_out(uv, ln_g, w_s_bf16, b_s_t, w_out_bf16, x, gate, ridx, tm=512):
    t, d = x.shape
    width = uv.shape[1] // 2
    return pl.pallas_call(
        _gmlp_out_kernel,
        grid=(t // tm,),
        in_specs=[
            pl.BlockSpec((tm, width), lambda i: (i, 0)),
            pl.BlockSpec((tm, width), lambda i: (i, 1)),
            pl.BlockSpec((1, width), lambda i: (0, 0)),
            pl.BlockSpec(w_s_bf16.shape, lambda i: (0, 0, 0)),
            pl.BlockSpec(b_s_t.shape, lambda i: (0, 0)),
            pl.BlockSpec((width, d), lambda i: (0, 0)),
            pl.BlockSpec((tm, d), lambda i: (i, 0)),
            pl.BlockSpec((1, 1, d), lambda i: (ridx(i, tm), 0, 0)),
        ],
        out_specs=pl.BlockSpec((tm, d), lambda i: (i, 0)),
        out_shape=jax.ShapeDtypeStruct((t, d), F32),
        scratch_shapes=[pltpu.VMEM((tm, width), BF16)],
        compiler_params=_cparams(56),
        name="gmlp_out",
    )(uv, uv, ln_g, w_s_bf16, b_s_t, w_out_bf16, x, gate)


def _qkv_kernel(x_ref, g_ref, sh_ref, sc_ref, w_ref, hg_ref, cos_ref, sin_ref, o_ref, h_ref, *, n_qk_tiles):
    j = pl.program_id(1)

    @pl.when(j == 0)
    def _():
        h_ref[...] = _rms_mod(x_ref[...], g_ref[...], sh_ref[0], sc_ref[0]).astype(BF16)

    acc = jnp.dot(h_ref[...], w_ref[...], preferred_element_type=F32)
    tm, tn = acc.shape

    @pl.when(j < n_qk_tiles)
    def _():
        cos = cos_ref[...]
        sin = sin_ref[...]
        first_half = (lax.broadcasted_iota(jnp.int32, (tm, HEAD_DIM), 1) % (HEAD_DIM // 2)) < (HEAD_DIM // 4)
        for hh in range(tn // HEAD_DIM):
            a = acc[:, hh * HEAD_DIM:(hh + 1) * HEAD_DIM]
            ms = jnp.mean(a * a, axis=-1, keepdims=True)
            a = a * lax.rsqrt(ms + NORM_EPS) * hg_ref[:, hh * HEAD_DIM:(hh + 1) * HEAD_DIM]
            partner = jnp.where(first_half,
                                pltpu.roll(a, HEAD_DIM - HEAD_DIM // 4, 1),
                                pltpu.roll(a, HEAD_DIM // 4, 1))
            o_ref[:, hh * HEAD_DIM:(hh + 1) * HEAD_DIM] = (a * cos + partner * sin).astype(o_ref.dtype)

    @pl.when(j >= n_qk_tiles)
    def _():
        o_ref[...] = acc.astype(o_ref.dtype)


def _qkv(x_all, t_rows, t_lat, g, shift, scale, w_bf16, head_gain, cos_t, sin_t, ridx, seq, tm=1024, tn=1024):
    d = x_all.shape[1]
    n = w_bf16.shape[1]
    n_qk_tiles = (2 * n // 3) // tn
    seq_blocks = seq // tm
    n_lat_blocks = t_lat // tm

    def rope_idx(i, j):
        return (jnp.where(i < n_lat_blocks, i % seq_blocks, seq_blocks), 0)

    return pl.pallas_call(
        functools.partial(_qkv_kernel, n_qk_tiles=n_qk_tiles),
        grid=(t_rows // tm, n // tn),
        in_specs=[
            pl.BlockSpec((tm, d), lambda i, j: (i, 0)),
            pl.BlockSpec((1, d), lambda i, j: (0, 0)),
            pl.BlockSpec((1, 1, d), lambda i, j: (ridx(i, tm), 0, 0)),
            pl.BlockSpec((1, 1, d), lambda i, j: (ridx(i, tm), 0, 0)),
            pl.BlockSpec((d, tn), lambda i, j: (0, j)),
            pl.BlockSpec((1, tn), lambda i, j: (0, j)),
            pl.BlockSpec((tm, HEAD_DIM), rope_idx),
            pl.BlockSpec((tm, HEAD_DIM), rope_idx),
        ],
        out_specs=pl.BlockSpec((tm, tn), lambda i, j: (i, j)),
        out_shape=jax.ShapeDtypeStruct((t_rows, n), BF16),
        scratch_shapes=[pltpu.VMEM((tm, d), BF16)],
        compiler_params=_cparams(48),
        name="qkv_proj",
    )(x_all, g, shift, scale, w_bf16, head_gain, cos_t, sin_t)


def _attn_kernel(lam_ref, q1_ref, q2_ref, k1l_ref, k2l_ref, k1c_ref, k2c_ref, vl_ref, vc_ref, sg_ref, o_ref,
                 *, out_scale):
    lam = lam_ref[0]
    scale = HEAD_DIM ** -0.5
    nt = (((1,), (1,)), ((), ()))

    def probs(q_ref, kc_ref, kl_ref):
        q = q_ref[...]
        sc = lax.dot_general(q, kc_ref[...], nt, preferred_element_type=F32) * scale
        sl = lax.dot_general(q, kl_ref[...], nt, preferred_element_type=F32) * scale
        m = jnp.maximum(jnp.max(sc, axis=-1, keepdims=True), jnp.max(sl, axis=-1, keepdims=True))
        pc = jnp.exp(sc - m)
        pl_ = jnp.exp(sl - m)
        inv = 1.0 / (jnp.sum(pc, axis=-1, keepdims=True) + jnp.sum(pl_, axis=-1, keepdims=True))
        return pc * inv, pl_ * inv

    p1c, p1l = probs(q1_ref, k1c_ref, k1l_ref)
    p2c, p2l = probs(q2_ref, k2c_ref, k2l_ref)
    ac = (p1c - lam * p2c).astype(BF16)
    al = (p1l - lam * p2l).astype(BF16)
    o = (jnp.dot(ac, vc_ref[...], preferred_element_type=F32)
         + jnp.dot(al, vl_ref[...], preferred_element_type=F32))
    ms = jnp.mean(o * o, axis=-1, keepdims=True)
    o = o * lax.rsqrt(ms + NORM_EPS) * sg_ref[...] * out_scale
    o_ref[...] = o.astype(o_ref.dtype)


def _diff_attention(qkv, lam, subln_g, batch, seq, ctx_len, d, out_scale, tq=512):
    t_lat = batch * seq
    vd = 2 * HEAD_DIM
    qb = seq // tq
    k_col0 = d // HEAD_DIM
    v_col0 = 2 * d // vd
    ctx_row0 = t_lat // ctx_len

    return pl.pallas_call(
        functools.partial(_attn_kernel, out_scale=out_scale),
        grid=(batch, DIFF_HEADS, qb),
        in_specs=[
            pl.BlockSpec(memory_space=pltpu.SMEM),
            pl.BlockSpec((tq, HEAD_DIM), lambda b, h, q: (b * qb + q, 2 * h)),
            pl.BlockSpec((tq, HEAD_DIM), lambda b, h, q: (b * qb + q, 2 * h + 1)),
            pl.BlockSpec((seq, HEAD_DIM), lambda b, h, q: (b, k_col0 + 2 * h)),
            pl.BlockSpec((seq, HEAD_DIM), lambda b, h, q: (b, k_col0 + 2 * h + 1)),
            pl.BlockSpec((ctx_len, HEAD_DIM), lambda b, h, q: (ctx_row0 + b, k_col0 + 2 * h)),
            pl.BlockSpec((ctx_len, HEAD_DIM), lambda b, h, q: (ctx_row0 + b, k_col0 + 2 * h + 1)),
            pl.BlockSpec((seq, vd), lambda b, h, q: (b, v_col0 + h)),
            pl.BlockSpec((ctx_len, vd), lambda b, h, q: (ctx_row0 + b, v_col0 + h)),
            pl.BlockSpec((1, vd), lambda b, h, q: (0, 0)),
        ],
        out_specs=pl.BlockSpec((tq, vd), lambda b, h, q: (b * qb + q, h)),
        out_shape=jax.ShapeDtypeStruct((t_lat, d), BF16),
        compiler_params=_cparams(56),
        name="diff_attention",
    )(lam, qkv, qkv, qkv, qkv, qkv, qkv, qkv, qkv, subln_g)


def _proj_residual_kernel(a_ref, w_ref, x_ref, gate_ref, o_ref):
    y = jnp.dot(a_ref[...], w_ref[...], preferred_element_type=F32)
    o_ref[...] = x_ref[...] + gate_ref[0] * y


def _proj_residual(a, w_bf16, x_all, gate, ridx, tm=1024, tn=1024):
    t, k = a.shape
    n = w_bf16.shape[1]
    return pl.pallas_call(
        _proj_residual_kernel,
        grid=(t // tm, n // tn),
        in_specs=[
            pl.BlockSpec((tm, k), lambda i, j: (i, 0)),
            pl.BlockSpec((k, tn), lambda i, j: (0, j)),
            pl.BlockSpec((tm, tn), lambda i, j: (i, j)),
            pl.BlockSpec((1, 1, tn), lambda i, j: (ridx(i, tm), 0, j)),
        ],
        out_specs=pl.BlockSpec((tm, tn), lambda i, j: (i, j)),
        out_shape=jax.ShapeDtypeStruct((t, n), F32),
        compiler_params=_cparams(48),
        name="attn_out_proj",
    )(a, w_bf16, x_all, gate)


def _router_kernel(x_ref, g_ref, sh_ref, sc_ref, wr_ref, br_ref, h_ref, idx_ref, wt_ref):
    h = _rms_mod(x_ref[...], g_ref[...], sh_ref[0], sc_ref[0])
    h_ref[...] = h
    logits = lax.dot_general(wr_ref[...], h, (((1,), (1,)), ((), ())),
                             precision=lax.Precision.HIGHEST, preferred_element_type=F32) + br_ref[...]
    n_e, tm = logits.shape
    e_iota = lax.broadcasted_iota(jnp.int32, (n_e, tm), 0).astype(F32)
    vals, idxs = [], []
    for _ in range(TOP_K):
        m = jnp.max(logits, axis=0, keepdims=True)
        idx = jnp.min(jnp.where(logits == m, e_iota, float(n_e)), axis=0, keepdims=True)
        vals.append(m)
        idxs.append(idx)
        logits = jnp.where(e_iota == idx, -jnp.inf, logits)
    exps = [jnp.exp(v - vals[0]) for v in vals]
    denom = exps[0]
    for e in exps[1:]:
        denom = denom + e
    idx_ref[...] = jnp.zeros(idx_ref.shape, jnp.int32)
    wt_ref[...] = jnp.zeros(wt_ref.shape, F32)
    for k in range(TOP_K):
        idx_ref[k:k + 1, :] = idxs[k].astype(jnp.int32)
        wt_ref[k:k + 1, :] = exps[k] / denom


def _router(x_all, t_rows, g, shift, scale, w_r_t, b_r, ridx, tm=512):
    d = x_all.shape[1]
    n_e = w_r_t.shape[0]
    return pl.pallas_call(
        _router_kernel,
        grid=(t_rows // tm,),
        in_specs=[
            pl.BlockSpec((tm, d), lambda i: (i, 0)),
            pl.BlockSpec((1, d), lambda i: (0, 0)),
            pl.BlockSpec((1, 1, d), lambda i: (ridx(i, tm), 0, 0)),
            pl.BlockSpec((1, 1, d), lambda i: (ridx(i, tm), 0, 0)),
            pl.BlockSpec((n_e, d), lambda i: (0, 0)),
            pl.BlockSpec((n_e, 1), lambda i: (0, 0)),
        ],
        out_specs=[
            pl.BlockSpec((tm, d), lambda i: (i, 0)),
            pl.BlockSpec((8, tm), lambda i: (0, i)),
            pl.BlockSpec((8, tm), lambda i: (0, i)),
        ],
        out_shape=[
            jax.ShapeDtypeStruct((t_rows, d), F32),
            jax.ShapeDtypeStruct((8, t_rows), jnp.int32),
            jax.ShapeDtypeStruct((8, t_rows), F32),
        ],
        compiler_params=_cparams(40),
        name="moe_router",
    )(x_all, g, shift, scale, w_r_t, b_r)


def _dispatch_kernel(zstart_ref, zcount_ref, n_used_ref, dest_ref, h_ref, xs_ref, zero_ref,
                     row_sem, zero_sem, tail_sem):
    i = pl.program_id(0)
    tm = h_ref.shape[0]
    n_blocks = xs_ref.shape[0] // EXPERT_ROWS

    def zero_row_copy(dst_row):
        return pltpu.make_async_copy(zero_ref.at[pl.ds(0, 1)], xs_ref.at[pl.ds(dst_row, 1)], zero_sem)

    def zero_block_copy(blk):
        start = pl.multiple_of(blk * EXPERT_ROWS, EXPERT_ROWS)
        return pltpu.make_async_copy(zero_ref, xs_ref.at[pl.ds(start, EXPERT_ROWS)], tail_sem)

    def scatter_row_copy(src_row, dst_row):
        return pltpu.make_async_copy(h_ref.at[pl.ds(src_row, 1)], xs_ref.at[pl.ds(dst_row, 1)], row_sem)

    @pl.when(i == 0)
    def _():
        zero_ref[...] = jnp.zeros(zero_ref.shape, zero_ref.dtype)

    tail_blk = n_used_ref[0] + (i - N_EXPERTS)
    zero_tail = jnp.logical_and(i >= N_EXPERTS, tail_blk < n_blocks)

    @pl.when(zero_tail)
    def _():
        zero_block_copy(tail_blk).start()

    @pl.when(i < N_EXPERTS)
    def _():
        off = zstart_ref[i]

        def issue_zero(r, carry):
            zero_row_copy(off + r).start()
            return carry

        lax.fori_loop(0, zcount_ref[i], issue_zero, 0)

    def issue(r, carry):
        for k in range(TOP_K):
            scatter_row_copy(r, dest_ref[0, 0, k * tm + r]).start()
        return carry

    lax.fori_loop(0, tm, issue, 0)

    def drain(r, carry):
        for k in range(TOP_K):
            scatter_row_copy(0, 0).wait()
        return carry

    lax.fori_loop(0, tm, drain, 0)

    @pl.when(i < N_EXPERTS)
    def _():
        def drain_zero(r, carry):
            zero_row_copy(0).wait()
            return carry

        lax.fori_loop(0, zcount_ref[i], drain_zero, 0)

    @pl.when(zero_tail)
    def _():
        zero_block_copy(tail_blk).wait()


def _dispatch(h, dest_blocks, zstart, zcount, n_used, n_pad, tm):
    t, d = h.shape
    assert t // tm >= 2 * N_EXPERTS
    grid_spec = pltpu.PrefetchScalarGridSpec(
        num_scalar_prefetch=3,
        grid=(t // tm,),
        in_specs=[
            pl.BlockSpec((1, 1, TOP_K * tm), lambda i, zs, zc, nu: (i, 0, 0), memory_space=pltpu.SMEM),
            pl.BlockSpec((tm, d), lambda i, zs, zc, nu: (i, 0)),
        ],
        out_specs=pl.BlockSpec(memory_space=pl.ANY),
        scratch_shapes=[
            pltpu.VMEM((EXPERT_ROWS, d), h.dtype),
            pltpu.SemaphoreType.DMA,
            pltpu.SemaphoreType.DMA,
            pltpu.SemaphoreType.DMA,
        ],
    )
    return pl.pallas_call(
        _dispatch_kernel,
        grid_spec=grid_spec,
        out_shape=jax.ShapeDtypeStruct((n_pad, d), h.dtype),
        compiler_params=_cparams(32),
        name="moe_dispatch",
    )(zstart, zcount, n_used, dest_blocks, h)


def _expert_up_kernel(blk_e_ref, n_used_ref, x_ref, wg_ref, wl_ref, bg_ref, bl_ref, o_ref, *, nc):
    i = pl.program_id(0)

    @pl.when(i < n_used_ref[0])
    def _():
        xb = x_ref[...].astype(BF16)
        de = o_ref.shape[1]
        for c in range(de // nc):
            sl = slice(c * nc, (c + 1) * nc)
            gate = jnp.dot(xb, wg_ref[0, :, sl], preferred_element_type=F32) + bg_ref[0, :, sl]
            lin = jnp.dot(xb, wl_ref[0, :, sl], preferred_element_type=F32) + bl_ref[0, :, sl]
            glu = jnp.minimum(gate, SWIGLU_LIMIT)
            lin = jnp.clip(lin, -SWIGLU_LIMIT, SWIGLU_LIMIT)
            act = glu * jax.nn.sigmoid(SWIGLU_ALPHA * glu) * (lin + 1.0)
            o_ref[:, sl] = act.astype(o_ref.dtype)

    @pl.when(i >= n_used_ref[0])
    def _():
        o_ref[...] = jnp.zeros(o_ref.shape, o_ref.dtype)


def _expert_up(xs, blk_e, n_used, wg, wl, bg, bl, nc=512):
    n_pad, d = xs.shape
    de = wg.shape[2]
    n_blocks = n_pad // EXPERT_ROWS

    def row_idx(i, be, nu):
        return (jnp.minimum(i, nu[0] - 1), 0)

    def w_idx(i, be, nu):
        return (be[i], 0, 0)

    grid_spec = pltpu.PrefetchScalarGridSpec(
        num_scalar_prefetch=2,
        grid=(n_blocks,),
        in_specs=[
            pl.BlockSpec((EXPERT_ROWS, d), row_idx),
            pl.BlockSpec((1, d, de), w_idx),
            pl.BlockSpec((1, d, de), w_idx),
            pl.BlockSpec((1, 1, de), w_idx),
            pl.BlockSpec((1, 1, de), w_idx),
        ],
        out_specs=pl.BlockSpec((EXPERT_ROWS, de), lambda i, be, nu: (i, 0)),
    )
    return pl.pallas_call(
        functools.partial(_expert_up_kernel, nc=nc),
        grid_spec=grid_spec,
        out_shape=jax.ShapeDtypeStruct((n_pad, de), BF16),
        compiler_params=_cparams(56),
        name="moe_expert_up",
    )(blk_e, n_used, xs, wg, wl, bg, bl)


def _expert_down_kernel(blk_e_ref, n_used_ref, a_ref, wd_ref, bd_ref, o_ref):
    i = pl.program_id(0)

    @pl.when(i < n_used_ref[0])
    def _():
        o_ref[...] = jnp.dot(a_ref[...], wd_ref[0], preferred_element_type=F32) + bd_ref[0]

    @pl.when(i >= n_used_ref[0])
    def _():
        o_ref[...] = jnp.zeros(o_ref.shape, o_ref.dtype)


def _expert_down(act, blk_e, n_used, wd, bd):
    n_pad, de = act.shape
    d = wd.shape[2]
    n_blocks = n_pad // EXPERT_ROWS

    def row_idx(i, be, nu):
        return (jnp.minimum(i, nu[0] - 1), 0)

    def w_idx(i, be, nu):
        return (be[i], 0, 0)

    grid_spec = pltpu.PrefetchScalarGridSpec(
        num_scalar_prefetch=2,
        grid=(n_blocks,),
        in_specs=[
            pl.BlockSpec((EXPERT_ROWS, de), row_idx),
            pl.BlockSpec((1, de, d), w_idx),
            pl.BlockSpec((1, 1, d), w_idx),
        ],
        out_specs=pl.BlockSpec((EXPERT_ROWS, d), lambda i, be, nu: (i, 0)),
    )
    return pl.pallas_call(
        _expert_down_kernel,
        grid_spec=grid_spec,
        out_shape=jax.ShapeDtypeStruct((n_pad, d), F32),
        compiler_params=_cparams(40),
        name="moe_expert_down",
    )(blk_e, n_used, act, wd, bd)


def _combine_kernel(dest_ref, wt_ref, x_ref, gate_ref, ys_ref, o_ref, buf_ref, sem):
    tm = x_ref.shape[0]

    def issue(r, carry):
        for k in range(TOP_K):
            src = dest_ref[0, 0, k * tm + r]
            pltpu.make_async_copy(ys_ref.at[pl.ds(src, 1)], buf_ref.at[k, pl.ds(r, 1)], sem).start()
        return carry

    lax.fori_loop(0, tm, issue, 0)

    def drain(r, carry):
        for k in range(TOP_K):
            pltpu.make_async_copy(ys_ref.at[pl.ds(0, 1)], buf_ref.at[k, pl.ds(0, 1)], sem).wait()
        return carry

    lax.fori_loop(0, tm, drain, 0)

    acc = wt_ref[:, 0:1] * buf_ref[0]
    for k in range(1, TOP_K):
        acc = acc + wt_ref[:, k:k + 1] * buf_ref[k]
    o_ref[...] = x_ref[...] + gate_ref[0] * acc


def _combine(ys, dest_blocks, wt, x_all, t_rows, gate, ridx, tm):
    d = x_all.shape[1]
    return pl.pallas_call(
        _combine_kernel,
        grid=(t_rows // tm,),
        in_specs=[
            pl.BlockSpec((1, 1, TOP_K * tm), lambda i: (i, 0, 0), memory_space=pltpu.SMEM),
            pl.BlockSpec((tm, TOP_K), lambda i: (i, 0)),
            pl.BlockSpec((tm, d), lambda i: (i, 0)),
            pl.BlockSpec((1, 1, d), lambda i: (ridx(i, tm), 0, 0)),
            pl.BlockSpec(memory_space=pl.ANY),
        ],
        out_specs=pl.BlockSpec((tm, d), lambda i: (i, 0)),
        out_shape=jax.ShapeDtypeStruct((t_rows, d), F32),
        scratch_shapes=[pltpu.VMEM((TOP_K, tm, d), F32), pltpu.SemaphoreType.DMA],
        compiler_params=_cparams(32),
        name="moe_combine",
    )(dest_blocks, wt, x_all, gate, ys)


def _slot_blocks(dest, tm):
    k, t = dest.shape
    return dest.reshape(k, t // tm, tm).transpose(1, 0, 2).reshape(t // tm, 1, k * tm)


def _moe(x_all, t_rows, norm_g, shift, scale, gate, ridx, w_r, b_r, w_gu, b_gu, w_down, b_down,
         dispatch_tm=256, combine_tm=128):
    d = x_all.shape[1]
    n_e = w_r.shape[1]
    h, top_idx, top_w = _router(x_all, t_rows, norm_g, shift, scale, w_r.T, b_r.reshape(n_e, 1), ridx)
    top_idx = top_idx[:TOP_K]
    top_w = top_w[:TOP_K]

    n_slots = TOP_K * t_rows
    e_flat = top_idx.reshape(n_slots)
    onehot = (e_flat[:, None] == jnp.arange(n_e, dtype=jnp.int32)[None, :]).astype(jnp.int32)
    csum = jnp.cumsum(onehot, axis=0)
    rank = jnp.sum(onehot * csum, axis=1) - 1
    counts = csum[-1]
    padded = (counts + EXPERT_ROWS - 1) // EXPERT_ROWS * EXPERT_ROWS
    pad_ends = jnp.cumsum(padded)
    pad_starts = pad_ends - padded
    dest = (jnp.sum(onehot * pad_starts[None, :], axis=1) + rank).reshape(TOP_K, t_rows)
    n_blocks = -(-(n_slots + n_e * (EXPERT_ROWS - 1)) // EXPERT_ROWS)
    n_pad = n_blocks * EXPERT_ROWS
    blk_e = jnp.minimum(
        jnp.searchsorted(pad_ends, jnp.arange(n_blocks, dtype=jnp.int32) * EXPERT_ROWS, side='right'),
        n_e - 1).astype(jnp.int32)
    n_used = (pad_ends[-1:] // EXPERT_ROWS).astype(jnp.int32)
    zstart = (pad_starts + counts).astype(jnp.int32)
    zcount = (padded - counts).astype(jnp.int32)

    xs = _dispatch(h, _slot_blocks(dest, dispatch_tm), zstart, zcount, n_used, n_pad, dispatch_tm)
    wg = w_gu[:, :, 0::2].astype(BF16)
    wl = w_gu[:, :, 1::2].astype(BF16)
    de = wg.shape[2]
    bg = b_gu[:, 0::2].reshape(n_e, 1, de)
    bl = b_gu[:, 1::2].reshape(n_e, 1, de)
    act = _expert_up(xs, blk_e, n_used, wg, wl, bg, bl)
    ys = _expert_down(act, blk_e, n_used, w_down.astype(BF16), b_down.reshape(n_e, 1, d))
    return _combine(ys, _slot_blocks(dest, combine_tm), top_w.T, x_all, t_rows, gate, ridx, combine_tm)


def _rope_tables(seq, pad_rows):
    rows = seq // GRID_W
    row_pos = jnp.repeat(jnp.arange(rows), GRID_W).astype(F32)
    col_pos = jnp.tile(jnp.arange(GRID_W), rows).astype(F32)
    n_freq = HEAD_DIM // 4
    inv_freq = ROPE_THETA ** (-jnp.arange(n_freq, dtype=F32) / n_freq)
    ang_r = row_pos[:, None] * inv_freq
    ang_c = col_pos[:, None] * inv_freq
    cos = jnp.concatenate([jnp.cos(ang_r)] * 2 + [jnp.cos(ang_c)] * 2, axis=1)
    sin = jnp.concatenate([-jnp.sin(ang_r), jnp.sin(ang_r), -jnp.sin(ang_c), jnp.sin(ang_c)], axis=1)
    cos = jnp.concatenate([cos, jnp.ones((pad_rows, HEAD_DIM), F32)], axis=0)
    sin = jnp.concatenate([sin, jnp.zeros((pad_rows, HEAD_DIM), F32)], axis=0)
    return cos, sin


def kernel(x, c, ctx, c_ctx, w_mod, b_mod, norm_mix_g, norm_ffn_g, gmlp_w_in, gmlp_norm_g, gmlp_w_s, gmlp_b_s,
           gmlp_w_out, diff_w_qkv, diff_q_norm_g, diff_k_norm_g, diff_lambda, diff_subln_g, diff_w_o,
           moe_w_router, moe_b_router, moe_w_gate_up, moe_b_gate_up, moe_w_down, moe_b_down):
    batch, seq, d = x.shape
    ctx_len = ctx.shape[1]
    depth = w_mod.shape[0]
    t_lat = batch * seq
    t_ctx = batch * ctx_len
    t_all = t_lat + t_ctx
    assert batch < MOD_ROWS and depth == 2

    def ridx(i, tm):
        return jnp.minimum((i * tm) // seq, batch)

    c16 = jnp.concatenate([c, c_ctx[None, :], jnp.zeros((MOD_ROWS - batch - 1, d), F32)], axis=0)
    mods = _modulation(c16, w_mod, b_mod).reshape(depth, MOD_ROWS, N_MOD, d)

    def mod(layer, k):
        return mods[layer, :, k, :].reshape(MOD_ROWS, 1, d)

    x_all = jnp.concatenate([x.reshape(t_lat, d), ctx.reshape(t_ctx, d)], axis=0)

    uv = _gmlp_in(x_all, norm_mix_g[0:1], mod(0, 0), mod(0, 1), gmlp_w_in[0].astype(BF16), ridx)
    x_all = _gmlp_out(uv, gmlp_norm_g[0:1], gmlp_w_s[0].astype(BF16), gmlp_b_s[0].T,
                      gmlp_w_out[0].astype(BF16), x_all, mod(0, 2), ridx)
    x_all = _moe(x_all, t_all, norm_ffn_g[0:1], mod(0, 3), mod(0, 4), mod(0, 5), ridx,
                 moe_w_router[0], moe_b_router[0], moe_w_gate_up[0], moe_b_gate_up[0],
                 moe_w_down[0], moe_b_down[0])

    lam_init = 0.8 - 0.6 * math.exp(-0.3 * 1)
    lf = diff_lambda[0].astype(F32)
    lam = (jnp.exp(jnp.sum(lf[0] * lf[1])) - jnp.exp(jnp.sum(lf[2] * lf[3])) + lam_init).reshape(1)
    tm_qkv = 1024
    cos_t, sin_t = _rope_tables(seq, tm_qkv)
    n_heads2 = d // HEAD_DIM
    head_gain = jnp.concatenate([jnp.tile(diff_q_norm_g[0], n_heads2), jnp.tile(diff_k_norm_g[0], n_heads2),
                                 jnp.ones((d,), F32)]).reshape(1, 3 * d)
    qkv = _qkv(x_all, t_all, t_lat, norm_mix_g[1:2], mod(1, 0), mod(1, 1), diff_w_qkv[0].astype(BF16), head_gain,
               cos_t, sin_t, ridx, seq, tm=tm_qkv)
    attn = _diff_attention(qkv, lam, diff_subln_g[0:1], batch, seq, ctx_len, d, 1.0 - lam_init)
    x_lat = _proj_residual(attn, diff_w_o[0].astype(BF16), x_all, mod(1, 2), ridx)
    x_lat = _moe(x_lat, t_lat, norm_ffn_g[1:2], mod(1, 3), mod(1, 4), mod(1, 5), ridx,
                 moe_w_router[1], moe_b_router[1], moe_w_gate_up[1], moe_b_gate_up[1],
                 moe_w_down[1], moe_b_down[1])
    return x_lat.reshape(batch, seq, d)
```

```python
import functools
import math

import jax
import jax.numpy as jnp
from jax import lax
from jax.experimental import pallas as pl
from jax.experimental.pallas import tpu as pltpu

F32 = jnp.float32
BF16 = jnp.bfloat16

NORM_EPS = 1e-6
N_MOD = 6
GRID_W = 64
GMLP_CHUNK = 128
GMLP_GROUPS = 8
DIFF_HEADS = 8
HEAD_DIM = 128
ROPE_THETA = 10000.0
N_EXPERTS = 32
TOP_K = 4
SWIGLU_ALPHA = 1.702
SWIGLU_LIMIT = 7.0

MOD_ROWS = 16
EXPERT_ROWS = 256
MIB = 1024 * 1024


def _cparams(vmem_mib):
    return pltpu.CompilerParams(vmem_limit_bytes=vmem_mib * MIB)


def _rms_mod(x, g, shift, scale):
    ms = jnp.mean(x * x, axis=-1, keepdims=True)
    y = x * lax.rsqrt(ms + NORM_EPS) * g
    return y * (1.0 + scale) + shift


def _mod_kernel(c_ref, w_ref, b_ref, o_ref):
    c = c_ref[...]
    s = c * jax.nn.sigmoid(c)
    o_ref[0] = jnp.dot(s.astype(BF16), w_ref[0].astype(BF16), preferred_element_type=F32) + b_ref[0]


def _modulation(c16, w_mod, b_mod):
    depth, d, n = w_mod.shape
    tn = 1024
    return pl.pallas_call(
        _mod_kernel,
        grid=(depth, n // tn),
        in_specs=[
            pl.BlockSpec((MOD_ROWS, d), lambda l, j: (0, 0)),
            pl.BlockSpec((1, d, tn), lambda l, j: (l, 0, j)),
            pl.BlockSpec((1, 1, tn), lambda l, j: (l, 0, j)),
        ],
        out_specs=pl.BlockSpec((1, MOD_ROWS, tn), lambda l, j: (l, 0, j)),
        out_shape=jax.ShapeDtypeStruct((depth, MOD_ROWS, n), F32),
        compiler_params=_cparams(40),
        name="modulation",
    )(c16, w_mod, b_mod.reshape(depth, 1, n))


def _gmlp_in_kernel(x_ref, g_ref, sh_ref, sc_ref, w_ref, o_ref, h_ref):
    @pl.when(pl.program_id(1) == 0)
    def _():
        h_ref[...] = _rms_mod(x_ref[...], g_ref[...], sh_ref[0], sc_ref[0]).astype(BF16)

    acc = jnp.dot(h_ref[...], w_ref[...], preferred_element_type=F32)
    gelu = 0.5 * acc * (1.0 + lax.erf(acc * (2.0 ** -0.5)))
    o_ref[...] = gelu.astype(o_ref.dtype)


def _gmlp_in(x, g, shift, scale, w_bf16, ridx, tm=1024, tn=1024):
    t, d = x.shape
    n = w_bf16.shape[1]
    return pl.pallas_call(
        _gmlp_in_kernel,
        grid=(t // tm, n // tn),
        in_specs=[
            pl.BlockSpec((tm, d), lambda i, j: (i, 0)),
            pl.BlockSpec((1, d), lambda i, j: (0, 0)),
            pl.BlockSpec((1, 1, d), lambda i, j: (ridx(i, tm), 0, 0)),
            pl.BlockSpec((1, 1, d), lambda i, j: (ridx(i, tm), 0, 0)),
            pl.BlockSpec((d, tn), lambda i, j: (0, j)),
        ],
        out_specs=pl.BlockSpec((tm, tn), lambda i, j: (i, j)),
        out_shape=jax.ShapeDtypeStruct((t, n), BF16),
        scratch_shapes=[pltpu.VMEM((tm, d), BF16)],
        compiler_params=_cparams(48),
        name="gmlp_in",
    )(x, g, shift, scale, w_bf16)


def _gmlp_out_kernel(u_ref, v_ref, lng_ref, ws_ref, bs_ref, wo_ref, x_ref, gate_ref, o_ref, t_ref):
    tm, width = v_ref.shape
    gw = width // GMLP_GROUPS
    v = v_ref[...].astype(F32)
    mu = jnp.mean(v, axis=-1, keepdims=True)
    vc = v - mu
    var = jnp.mean(vc * vc, axis=-1, keepdims=True)
    vn = (vc * lax.rsqrt(var + NORM_EPS) * lng_ref[...]).astype(BF16)
    for c in range(tm // GMLP_CHUNK):
        r0 = c * GMLP_CHUNK
        for g in range(GMLP_GROUPS):
            c0 = g * gw
            mixed = jnp.dot(ws_ref[g], vn[r0:r0 + GMLP_CHUNK, c0:c0 + gw], preferred_element_type=F32)
            mixed = mixed + bs_ref[:, g:g + 1]
            u = u_ref[r0:r0 + GMLP_CHUNK, c0:c0 + gw].astype(F32)
            t_ref[r0:r0 + GMLP_CHUNK, c0:c0 + gw] = (u * mixed).astype(BF16)
    y = jnp.dot(t_ref[...], wo_ref[...], preferred_element_type=F32)
    o_ref[...] = x_ref[...] + gate_ref[0] * y


def _gmlp_out(uv, ln_g, w_s_bf16, b_s_t, w_out_bf16, x, gate, ridx, tm=512):
    t, d = x.shape
    width = uv.shape[1] // 2
    return pl.pallas_call(
        _gmlp_out_kernel,
        grid=(t // tm,),
        in_specs=[
            pl.BlockSpec((tm, width), lambda i: (i, 0)),
            pl.BlockSpec((tm, width), lambda i: (i, 1)),
            pl.BlockSpec((1, width), lambda i: (0, 0)),
            pl.BlockSpec(w_s_bf16.shape, lambda i: (0, 0, 0)),
            pl.BlockSpec(b_s_t.shape, lambda i: (0, 0)),
            pl.BlockSpec((width, d), lambda i: (0, 0)),
            pl.BlockSpec((tm, d), lambda i: (i, 0)),
            pl.BlockSpec((1, 1, d), lambda i: (ridx(i, tm), 0, 0)),
        ],
        out_specs=pl.BlockSpec((tm, d), lambda i: (i, 0)),
        out_shape=jax.ShapeDtypeStruct((t, d), F32),
        scratch_shapes=[pltpu.VMEM((tm, width), BF16)],
        compiler_params=_cparams(56),
        name="gmlp_out",
    )(uv, uv, ln_g, w_s_bf16, b_s_t, w_out_bf16, x, gate)


def _qkv_kernel(x_ref, g_ref, sh_ref, sc_ref, w_ref, hg_ref, cos_ref, sin_ref, o_ref, h_ref, *, n_qk_tiles):
    j = pl.program_id(1)

    @pl.when(j == 0)
    def _():
        h_ref[...] = _rms_mod(x_ref[...], g_ref[...], sh_ref[0], sc_ref[0]).astype(BF16)

    acc = jnp.dot(h_ref[...], w_ref[...], preferred_element_type=F32)
    tm, tn = acc.shape

    @pl.when(j < n_qk_tiles)
    def _():
        cos = cos_ref[...]
        sin = sin_ref[...]
        first_half = (lax.broadcasted_iota(jnp.int32, (tm, HEAD_DIM), 1) % (HEAD_DIM // 2)) < (HEAD_DIM // 4)
        for hh in range(tn // HEAD_DIM):
            a = acc[:, hh * HEAD_DIM:(hh + 1) * HEAD_DIM]
            ms = jnp.mean(a * a, axis=-1, keepdims=True)
            a = a * lax.rsqrt(ms + NORM_EPS) * hg_ref[:, hh * HEAD_DIM:(hh + 1) * HEAD_DIM]
            partner = jnp.where(first_half,
                                pltpu.roll(a, HEAD_DIM - HEAD_DIM // 4, 1),
                                pltpu.roll(a, HEAD_DIM // 4, 1))
            o_ref[:, hh * HEAD_DIM:(hh + 1) * HEAD_DIM] = (a * cos + partner * sin).astype(o_ref.dtype)

    @pl.when(j >= n_qk_tiles)
    def _():
        o_ref[...] = acc.astype(o_ref.dtype)


def _qkv(x_all, t_rows, t_lat, g, shift, scale, w_bf16, head_gain, cos_t, sin_t, ridx, seq, tm=1024, tn=1024):
    d = x_all.shape[1]
    n = w_bf16.shape[1]
    n_qk_tiles = (2 * n // 3) // tn
    seq_blocks = seq // tm
    n_lat_blocks = t_lat // tm

    def rope_idx(i, j):
        return (jnp.where(i < n_lat_blocks, i % seq_blocks, seq_blocks), 0)

    return pl.pallas_call(
        functools.partial(_qkv_kernel, n_qk_tiles=n_qk_tiles),
        grid=(t_rows // tm, n // tn),
        in_specs=[
            pl.BlockSpec((tm, d), lambda i, j: (i, 0)),
            pl.BlockSpec((1, d), lambda i, j: (0, 0)),
            pl.BlockSpec((1, 1, d), lambda i, j: (ridx(i, tm), 0, 0)),
            pl.BlockSpec((1, 1, d), lambda i, j: (ridx(i, tm), 0, 0)),
            pl.BlockSpec((d, tn), lambda i, j: (0, j)),
            pl.BlockSpec((1, tn), lambda i, j: (0, j)),
            pl.BlockSpec((tm, HEAD_DIM), rope_idx),
            pl.BlockSpec((tm, HEAD_DIM), rope_idx),
        ],
        out_specs=pl.BlockSpec((tm, tn), lambda i, j: (i, j)),
        out_shape=jax.ShapeDtypeStruct((t_rows, n), BF16),
        scratch_shapes=[pltpu.VMEM((tm, d), BF16)],
        compiler_params=_cparams(48),
        name="qkv_proj",
    )(x_all, g, shift, scale, w_bf16, head_gain, cos_t, sin_t)


def _attn_kernel(lam_ref, q1_ref, q2_ref, k1l_ref, k2l_ref, k1c_ref, k2c_ref, vl_ref, vc_ref, sg_ref, o_ref,
                 *, out_scale):
    lam = lam_ref[0]
    nt = (((1,), (1,)), ((), ()))

    def softmax_times_v(q_ref, kc_ref, kl_ref):
        q = q_ref[...]
        sc = lax.dot_general(q, kc_ref[...], nt, preferred_element_type=F32)
        sl = lax.dot_general(q, kl_ref[...], nt, preferred_element_type=F32)
        m = jnp.maximum(jnp.max(sc, axis=-1, keepdims=True), jnp.max(sl, axis=-1, keepdims=True))
        pc = jnp.exp2(sc - m)
        pl_ = jnp.exp2(sl - m)
        z = jnp.sum(pc, axis=-1, keepdims=True) + jnp.sum(pl_, axis=-1, keepdims=True)
        pv = (jnp.dot(pc.astype(BF16), vc_ref[...], preferred_element_type=F32)
              + jnp.dot(pl_.astype(BF16), vl_ref[...], preferred_element_type=F32))
        return pv / z

    o = softmax_times_v(q1_ref, k1c_ref, k1l_ref) - lam * softmax_times_v(q2_ref, k2c_ref, k2l_ref)
    ms = jnp.mean(o * o, axis=-1, keepdims=True)
    o = o * lax.rsqrt(ms + NORM_EPS) * sg_ref[...] * out_scale
    o_ref[...] = o.astype(o_ref.dtype)


def _diff_attention(qkv, lam, subln_g, batch, seq, ctx_len, d, out_scale, tq=512):
    t_lat = batch * seq
    vd = 2 * HEAD_DIM
    qb = seq // tq
    k_col0 = d // HEAD_DIM
    v_col0 = 2 * d // vd
    ctx_row0 = t_lat // ctx_len

    return pl.pallas_call(
        functools.partial(_attn_kernel, out_scale=out_scale),
        grid=(batch, DIFF_HEADS, qb),
        in_specs=[
            pl.BlockSpec(memory_space=pltpu.SMEM),
            pl.BlockSpec((tq, HEAD_DIM), lambda b, h, q: (b * qb + q, 2 * h)),
            pl.BlockSpec((tq, HEAD_DIM), lambda b, h, q: (b * qb + q, 2 * h + 1)),
            pl.BlockSpec((seq, HEAD_DIM), lambda b, h, q: (b, k_col0 + 2 * h)),
            pl.BlockSpec((seq, HEAD_DIM), lambda b, h, q: (b, k_col0 + 2 * h + 1)),
            pl.BlockSpec((ctx_len, HEAD_DIM), lambda b, h, q: (ctx_row0 + b, k_col0 + 2 * h)),
            pl.BlockSpec((ctx_len, HEAD_DIM), lambda b, h, q: (ctx_row0 + b, k_col0 + 2 * h + 1)),
            pl.BlockSpec((seq, vd), lambda b, h, q: (b, v_col0 + h)),
            pl.BlockSpec((ctx_len, vd), lambda b, h, q: (ctx_row0 + b, v_col0 + h)),
            pl.BlockSpec((1, vd), lambda b, h, q: (0, 0)),
        ],
        out_specs=pl.BlockSpec((tq, vd), lambda b, h, q: (b * qb + q, h)),
        out_shape=jax.ShapeDtypeStruct((t_lat, d), BF16),
        compiler_params=_cparams(56),
        name="diff_attention",
    )(lam, qkv, qkv, qkv, qkv, qkv, qkv, qkv, qkv, subln_g)


def _proj_residual_kernel(a_ref, w_ref, x_ref, gate_ref, o_ref):
    y = jnp.dot(a_ref[...], w_ref[...], preferred_element_type=F32)
    o_ref[...] = x_ref[...] + gate_ref[0] * y


def _proj_residual(a, w_bf16, x_all, gate, ridx, tm=1024, tn=1024):
    t, k = a.shape
    n = w_bf16.shape[1]
    return pl.pallas_call(
        _proj_residual_kernel,
        grid=(t // tm, n // tn),
        in_specs=[
            pl.BlockSpec((tm, k), lambda i, j: (i, 0)),
            pl.BlockSpec((k, tn), lambda i, j: (0, j)),
            pl.BlockSpec((tm, tn), lambda i, j: (i, j)),
            pl.BlockSpec((1, 1, tn), lambda i, j: (ridx(i, tm), 0, j)),
        ],
        out_specs=pl.BlockSpec((tm, tn), lambda i, j: (i, j)),
        out_shape=jax.ShapeDtypeStruct((t, n), F32),
        compiler_params=_cparams(48),
        name="attn_out_proj",
    )(a, w_bf16, x_all, gate)


def _router_kernel(x_ref, g_ref, sh_ref, sc_ref, wr_ref, br_ref, h_ref, idx_ref, wt_ref):
    h = _rms_mod(x_ref[...], g_ref[...], sh_ref[0], sc_ref[0])
    h_ref[...] = h
    logits = lax.dot_general(wr_ref[...], h, (((1,), (1,)), ((), ())),
                             precision=lax.Precision.HIGHEST, preferred_element_type=F32) + br_ref[...]
    n_e, tm = logits.shape
    e_iota = lax.broadcasted_iota(jnp.int32, (n_e, tm), 0).astype(F32)
    vals, idxs = [], []
    for _ in range(TOP_K):
        m = jnp.max(logits, axis=0, keepdims=True)
        idx = jnp.min(jnp.where(logits == m, e_iota, float(n_e)), axis=0, keepdims=True)
        vals.append(m)
        idxs.append(idx)
        logits = jnp.where(e_iota == idx, -jnp.inf, logits)
    exps = [jnp.exp(v - vals[0]) for v in vals]
    denom = exps[0]
    for e in exps[1:]:
        denom = denom + e
    idx_ref[...] = jnp.zeros(idx_ref.shape, jnp.int32)
    wt_ref[...] = jnp.zeros(wt_ref.shape, F32)
    for k in range(TOP_K):
        idx_ref[k:k + 1, :] = idxs[k].astype(jnp.int32)
        wt_ref[k:k + 1, :] = exps[k] / denom


def _router(x_all, t_rows, g, shift, scale, w_r_t, b_r, ridx, tm=512):
    d = x_all.shape[1]
    n_e = w_r_t.shape[0]
    return pl.pallas_call(
        _router_kernel,
        grid=(t_rows // tm,),
        in_specs=[
            pl.BlockSpec((tm, d), lambda i: (i, 0)),
            pl.BlockSpec((1, d), lambda i: (0, 0)),
            pl.BlockSpec((1, 1, d), lambda i: (ridx(i, tm), 0, 0)),
            pl.BlockSpec((1, 1, d), lambda i: (ridx(i, tm), 0, 0)),
            pl.BlockSpec((n_e, d), lambda i: (0, 0)),
            pl.BlockSpec((n_e, 1), lambda i: (0, 0)),
        ],
        out_specs=[
            pl.BlockSpec((tm, d), lambda i: (i, 0)),
            pl.BlockSpec((8, tm), lambda i: (0, i)),
            pl.BlockSpec((8, tm), lambda i: (0, i)),
        ],
        out_shape=[
            jax.ShapeDtypeStruct((t_rows, d), F32),
            jax.ShapeDtypeStruct((8, t_rows), jnp.int32),
            jax.ShapeDtypeStruct((8, t_rows), F32),
        ],
        compiler_params=_cparams(40),
        name="moe_router",
    )(x_all, g, shift, scale, w_r_t, b_r)


def _dispatch_kernel(zstart_ref, zcount_ref, n_used_ref, dest_ref, h_ref, xs_ref, zero_ref,
                     row_sem, zero_sem, tail_sem):
    i = pl.program_id(0)
    tm = h_ref.shape[0]
    n_blocks = xs_ref.shape[0] // EXPERT_ROWS

    def zero_row_copy(dst_row):
        return pltpu.make_async_copy(zero_ref.at[pl.ds(0, 1)], xs_ref.at[pl.ds(dst_row, 1)], zero_sem)

    def zero_block_copy(blk):
        start = pl.multiple_of(blk * EXPERT_ROWS, EXPERT_ROWS)
        return pltpu.make_async_copy(zero_ref, xs_ref.at[pl.ds(start, EXPERT_ROWS)], tail_sem)

    def scatter_row_copy(src_row, dst_row):
        return pltpu.make_async_copy(h_ref.at[pl.ds(src_row, 1)], xs_ref.at[pl.ds(dst_row, 1)], row_sem)

    @pl.when(i == 0)
    def _():
        zero_ref[...] = jnp.zeros(zero_ref.shape, zero_ref.dtype)

    tail_blk = n_used_ref[0] + (i - N_EXPERTS)
    zero_tail = jnp.logical_and(i >= N_EXPERTS, tail_blk < n_blocks)

    @pl.when(zero_tail)
    def _():
        zero_block_copy(tail_blk).start()

    @pl.when(i < N_EXPERTS)
    def _():
        off = zstart_ref[i]

        def issue_zero(r, carry):
            zero_row_copy(off + r).start()
            return carry

        lax.fori_loop(0, zcount_ref[i], issue_zero, 0)

    def issue(r, carry):
        for k in range(TOP_K):
            scatter_row_copy(r, dest_ref[0, 0, k * tm + r]).start()
        return carry

    lax.fori_loop(0, tm, issue, 0)

    def drain(r, carry):
        for k in range(TOP_K):
            scatter_row_copy(0, 0).wait()
        return carry

    lax.fori_loop(0, tm, drain, 0)

    @pl.when(i < N_EXPERTS)
    def _():
        def drain_zero(r, carry):
            zero_row_copy(0).wait()
            return carry

        lax.fori_loop(0, zcount_ref[i], drain_zero, 0)

    @pl.when(zero_tail)
    def _():
        zero_block_copy(tail_blk).wait()


def _dispatch(h, dest_blocks, zstart, zcount, n_used, n_pad, tm):
    t, d = h.shape
    assert t // tm >= 2 * N_EXPERTS
    grid_spec = pltpu.PrefetchScalarGridSpec(
        num_scalar_prefetch=3,
        grid=(t // tm,),
        in_specs=[
            pl.BlockSpec((1, 1, TOP_K * tm), lambda i, zs, zc, nu: (i, 0, 0), memory_space=pltpu.SMEM),
            pl.BlockSpec((tm, d), lambda i, zs, zc, nu: (i, 0)),
        ],
        out_specs=pl.BlockSpec(memory_space=pl.ANY),
        scratch_shapes=[
            pltpu.VMEM((EXPERT_ROWS, d), h.dtype),
            pltpu.SemaphoreType.DMA,
            pltpu.SemaphoreType.DMA,
            pltpu.SemaphoreType.DMA,
        ],
    )
    return pl.pallas_call(
        _dispatch_kernel,
        grid_spec=grid_spec,
        out_shape=jax.ShapeDtypeStruct((n_pad, d), h.dtype),
        compiler_params=_cparams(32),
        name="moe_dispatch",
    )(zstart, zcount, n_used, dest_blocks, h)


PERM_TILE = 256


def _split_gate_up_kernel(w_ref, p_ref, og_ref, ol_ref):
    half = PERM_TILE // 2
    tn = w_ref.shape[2]
    for c in range(tn // PERM_TILE):
        w = w_ref[0, :, c * PERM_TILE:(c + 1) * PERM_TILE].astype(BF16)
        t = jnp.dot(w, p_ref[...], preferred_element_type=F32)
        og_ref[0, :, c * half:(c + 1) * half] = t[:, :half].astype(BF16)
        ol_ref[0, :, c * half:(c + 1) * half] = t[:, half:].astype(BF16)


def _split_gate_up(w_gu, tn=1024):
    n_e, d, n2 = w_gu.shape
    half = PERM_TILE // 2
    col = jnp.arange(PERM_TILE, dtype=jnp.int32)
    src = jnp.where(col < half, 2 * col, 2 * (col - half) + 1)
    perm = (jnp.arange(PERM_TILE, dtype=jnp.int32)[:, None] == src[None, :]).astype(BF16)
    out = jax.ShapeDtypeStruct((n_e, d, n2 // 2), BF16)
    return pl.pallas_call(
        _split_gate_up_kernel,
        grid=(n_e, n2 // tn),
        in_specs=[
            pl.BlockSpec((1, d, tn), lambda e, j: (e, 0, j)),
            pl.BlockSpec((PERM_TILE, PERM_TILE), lambda e, j: (0, 0)),
        ],
        out_specs=[
            pl.BlockSpec((1, d, tn // 2), lambda e, j: (e, 0, j)),
            pl.BlockSpec((1, d, tn // 2), lambda e, j: (e, 0, j)),
        ],
        out_shape=[out, out],
        compiler_params=_cparams(40),
        name="moe_split_gate_up",
    )(w_gu, perm)


def _expert_up_kernel(blk_e_ref, n_used_ref, x_ref, wg_ref, wl_ref, bg_ref, bl_ref, o_ref, *, nc):
    i = pl.program_id(0)

    @pl.when(i < n_used_ref[0])
    def _():
        xb = x_ref[...].astype(BF16)
        de = o_ref.shape[1]
        for c in range(de // nc):
            sl = slice(c * nc, (c + 1) * nc)
            gate = jnp.dot(xb, wg_ref[0, :, sl], preferred_element_type=F32) + bg_ref[0, :, sl]
            lin = jnp.dot(xb, wl_ref[0, :, sl], preferred_element_type=F32) + bl_ref[0, :, sl]
            glu = jnp.minimum(gate, SWIGLU_LIMIT)
            lin = jnp.clip(lin, -SWIGLU_LIMIT, SWIGLU_LIMIT)
            act = glu * jax.nn.sigmoid(SWIGLU_ALPHA * glu) * (lin + 1.0)
            o_ref[:, sl] = act.astype(o_ref.dtype)

    @pl.when(i >= n_used_ref[0])
    def _():
        o_ref[...] = jnp.zeros(o_ref.shape, o_ref.dtype)


def _expert_up(xs, blk_e, n_used, wg, wl, bg, bl, nc=512):
    n_pad, d = xs.shape
    de = wg.shape[2]
    n_blocks = n_pad // EXPERT_ROWS

    def row_idx(i, be, nu):
        return (jnp.minimum(i, nu[0] - 1), 0)

    def w_idx(i, be, nu):
        return (be[i], 0, 0)

    grid_spec = pltpu.PrefetchScalarGridSpec(
        num_scalar_prefetch=2,
        grid=(n_blocks,),
        in_specs=[
            pl.BlockSpec((EXPERT_ROWS, d), row_idx),
            pl.BlockSpec((1, d, de), w_idx),
            pl.BlockSpec((1, d, de), w_idx),
            pl.BlockSpec((1, 1, de), w_idx),
            pl.BlockSpec((1, 1, de), w_idx),
        ],
        out_specs=pl.BlockSpec((EXPERT_ROWS, de), lambda i, be, nu: (i, 0)),
    )
    return pl.pallas_call(
        functools.partial(_expert_up_kernel, nc=nc),
        grid_spec=grid_spec,
        out_shape=jax.ShapeDtypeStruct((n_pad, de), BF16),
        compiler_params=_cparams(56),
        name="moe_expert_up",
    )(blk_e, n_used, xs, wg, wl, bg, bl)


def _expert_down_kernel(blk_e_ref, n_used_ref, a_ref, wd_ref, bd_ref, o_ref, wb_ref):
    i = pl.program_id(0)
    new_expert = jnp.logical_or(i == 0, blk_e_ref[i] != blk_e_ref[jnp.maximum(i - 1, 0)])

    @pl.when(jnp.logical_and(i < n_used_ref[0], new_expert))
    def _():
        wb_ref[...] = wd_ref[0].astype(BF16)

    @pl.when(i < n_used_ref[0])
    def _():
        o_ref[...] = jnp.dot(a_ref[...], wb_ref[...], preferred_element_type=F32) + bd_ref[0]

    @pl.when(i >= n_used_ref[0])
    def _():
        o_ref[...] = jnp.zeros(o_ref.shape, o_ref.dtype)


def _expert_down(act, blk_e, n_used, wd, bd):
    n_pad, de = act.shape
    d = wd.shape[2]
    n_blocks = n_pad // EXPERT_ROWS

    def row_idx(i, be, nu):
        return (jnp.minimum(i, nu[0] - 1), 0)

    def w_idx(i, be, nu):
        return (be[i], 0, 0)

    grid_spec = pltpu.PrefetchScalarGridSpec(
        num_scalar_prefetch=2,
        grid=(n_blocks,),
        in_specs=[
            pl.BlockSpec((EXPERT_ROWS, de), row_idx),
            pl.BlockSpec((1, de, d), w_idx),
            pl.BlockSpec((1, 1, d), w_idx),
        ],
        out_specs=pl.BlockSpec((EXPERT_ROWS, d), lambda i, be, nu: (i, 0)),
        scratch_shapes=[pltpu.VMEM((de, d), BF16)],
    )
    return pl.pallas_call(
        _expert_down_kernel,
        grid_spec=grid_spec,
        out_shape=jax.ShapeDtypeStruct((n_pad, d), F32),
        compiler_params=_cparams(56),
        name="moe_expert_down",
    )(blk_e, n_used, act, wd, bd)


def _combine_kernel(dest_ref, wt_ref, x_ref, gate_ref, ys_ref, o_ref, buf_ref, sem):
    tm = x_ref.shape[0]

    def issue(r, carry):
        for k in range(TOP_K):
            src = dest_ref[0, 0, k * tm + r]
            pltpu.make_async_copy(ys_ref.at[pl.ds(src, 1)], buf_ref.at[k, pl.ds(r, 1)], sem).start()
        return carry

    lax.fori_loop(0, tm, issue, 0)

    def drain(r, carry):
        for k in range(TOP_K):
            pltpu.make_async_copy(ys_ref.at[pl.ds(0, 1)], buf_ref.at[k, pl.ds(0, 1)], sem).wait()
        return carry

    lax.fori_loop(0, tm, drain, 0)

    acc = wt_ref[:, 0:1] * buf_ref[0]
    for k in range(1, TOP_K):
        acc = acc + wt_ref[:, k:k + 1] * buf_ref[k]
    o_ref[...] = x_ref[...] + gate_ref[0] * acc


def _combine(ys, dest_blocks, wt, x_all, t_rows, gate, ridx, tm):
    d = x_all.shape[1]
    return pl.pallas_call(
        _combine_kernel,
        grid=(t_rows // tm,),
        in_specs=[
            pl.BlockSpec((1, 1, TOP_K * tm), lambda i: (i, 0, 0), memory_space=pltpu.SMEM),
            pl.BlockSpec((tm, TOP_K), lambda i: (i, 0)),
            pl.BlockSpec((tm, d), lambda i: (i, 0)),
            pl.BlockSpec((1, 1, d), lambda i: (ridx(i, tm), 0, 0)),
            pl.BlockSpec(memory_space=pl.ANY),
        ],
        out_specs=pl.BlockSpec((tm, d), lambda i: (i, 0)),
        out_shape=jax.ShapeDtypeStruct((t_rows, d), F32),
        scratch_shapes=[pltpu.VMEM((TOP_K, tm, d), F32), pltpu.SemaphoreType.DMA],
        compiler_params=_cparams(32),
        name="moe_combine",
    )(dest_blocks, wt, x_all, gate, ys)


def _slot_blocks(dest, tm):
    k, t = dest.shape
    return dest.reshape(k, t // tm, tm).transpose(1, 0, 2).reshape(t // tm, 1, k * tm)


def _moe(x_all, t_rows, norm_g, shift, scale, gate, ridx, w_r, b_r, w_gu, b_gu, w_down, b_down,
         dispatch_tm=256, combine_tm=128):
    d = x_all.shape[1]
    n_e = w_r.shape[1]
    h, top_idx, top_w = _router(x_all, t_rows, norm_g, shift, scale, w_r.T, b_r.reshape(n_e, 1), ridx)
    top_idx = top_idx[:TOP_K]
    top_w = top_w[:TOP_K]

    n_slots = TOP_K * t_rows
    e_flat = top_idx.reshape(n_slots)
    onehot = (e_flat[:, None] == jnp.arange(n_e, dtype=jnp.int32)[None, :]).astype(jnp.int32)
    csum = jnp.cumsum(onehot, axis=0)
    rank = jnp.sum(onehot * csum, axis=1) - 1
    counts = csum[-1]
    padded = (counts + EXPERT_ROWS - 1) // EXPERT_ROWS * EXPERT_ROWS
    pad_ends = jnp.cumsum(padded)
    pad_starts = pad_ends - padded
    dest = (jnp.sum(onehot * pad_starts[None, :], axis=1) + rank).reshape(TOP_K, t_rows)
    n_blocks = -(-(n_slots + n_e * (EXPERT_ROWS - 1)) // EXPERT_ROWS)
    n_pad = n_blocks * EXPERT_ROWS
    blk_row0 = jnp.arange(n_blocks, dtype=jnp.int32) * EXPERT_ROWS
    blk_e = jnp.minimum(jnp.sum((pad_ends[None, :] <= blk_row0[:, None]).astype(jnp.int32), axis=1), n_e - 1)
    n_used = (pad_ends[-1:] // EXPERT_ROWS).astype(jnp.int32)
    zstart = (pad_starts + counts).astype(jnp.int32)
    zcount = (padded - counts).astype(jnp.int32)

    xs = _dispatch(h, _slot_blocks(dest, dispatch_tm), zstart, zcount, n_used, n_pad, dispatch_tm)
    wg, wl = _split_gate_up(w_gu)
    de = wg.shape[2]
    bg = b_gu[:, 0::2].reshape(n_e, 1, de)
    bl = b_gu[:, 1::2].reshape(n_e, 1, de)
    act = _expert_up(xs, blk_e, n_used, wg, wl, bg, bl)
    ys = _expert_down(act, blk_e, n_used, w_down, b_down.reshape(n_e, 1, d))
    return _combine(ys, _slot_blocks(dest, combine_tm), top_w.T, x_all, t_rows, gate, ridx, combine_tm)


def _rope_tables(seq, pad_rows):
    rows = seq // GRID_W
    row_pos = jnp.repeat(jnp.arange(rows), GRID_W).astype(F32)
    col_pos = jnp.tile(jnp.arange(GRID_W), rows).astype(F32)
    n_freq = HEAD_DIM // 4
    inv_freq = ROPE_THETA ** (-jnp.arange(n_freq, dtype=F32) / n_freq)
    ang_r = row_pos[:, None] * inv_freq
    ang_c = col_pos[:, None] * inv_freq
    cos = jnp.concatenate([jnp.cos(ang_r)] * 2 + [jnp.cos(ang_c)] * 2, axis=1)
    sin = jnp.concatenate([-jnp.sin(ang_r), jnp.sin(ang_r), -jnp.sin(ang_c), jnp.sin(ang_c)], axis=1)
    cos = jnp.concatenate([cos, jnp.ones((pad_rows, HEAD_DIM), F32)], axis=0)
    sin = jnp.concatenate([sin, jnp.zeros((pad_rows, HEAD_DIM), F32)], axis=0)
    return cos, sin


def kernel(x, c, ctx, c_ctx, w_mod, b_mod, norm_mix_g, norm_ffn_g, gmlp_w_in, gmlp_norm_g, gmlp_w_s, gmlp_b_s,
           gmlp_w_out, diff_w_qkv, diff_q_norm_g, diff_k_norm_g, diff_lambda, diff_subln_g, diff_w_o,
           moe_w_router, moe_b_router, moe_w_gate_up, moe_b_gate_up, moe_w_down, moe_b_down):
    batch, seq, d = x.shape
    ctx_len = ctx.shape[1]
    depth = w_mod.shape[0]
    t_lat = batch * seq
    t_ctx = batch * ctx_len
    t_all = t_lat + t_ctx
    assert batch < MOD_ROWS and depth == 2

    def ridx(i, tm):
        return jnp.minimum((i * tm) // seq, batch)

    c16 = jnp.concatenate([c, c_ctx[None, :], jnp.zeros((MOD_ROWS - batch - 1, d), F32)], axis=0)
    mods = _modulation(c16, w_mod, b_mod).reshape(depth, MOD_ROWS, N_MOD, d)

    def mod(layer, k):
        return mods[layer, :, k, :].reshape(MOD_ROWS, 1, d)

    x_all = jnp.concatenate([x.reshape(t_lat, d), ctx.reshape(t_ctx, d)], axis=0)

    uv = _gmlp_in(x_all, norm_mix_g[0:1], mod(0, 0), mod(0, 1), gmlp_w_in[0].astype(BF16), ridx)
    x_all = _gmlp_out(uv, gmlp_norm_g[0:1], gmlp_w_s[0].astype(BF16), gmlp_b_s[0].T,
                      gmlp_w_out[0].astype(BF16), x_all, mod(0, 2), ridx)
    x_all = _moe(x_all, t_all, norm_ffn_g[0:1], mod(0, 3), mod(0, 4), mod(0, 5), ridx,
                 moe_w_router[0], moe_b_router[0], moe_w_gate_up[0], moe_b_gate_up[0],
                 moe_w_down[0], moe_b_down[0])

    lam_init = 0.8 - 0.6 * math.exp(-0.3 * 1)
    lf = diff_lambda[0].astype(F32)
    lam = (jnp.exp(jnp.sum(lf[0] * lf[1])) - jnp.exp(jnp.sum(lf[2] * lf[3])) + lam_init).reshape(1)
    tm_qkv = 1024
    cos_t, sin_t = _rope_tables(seq, tm_qkv)
    n_heads2 = d // HEAD_DIM
    q_scale = HEAD_DIM ** -0.5 * math.log2(math.e)
    head_gain = jnp.concatenate([jnp.tile(diff_q_norm_g[0] * q_scale, n_heads2),
                                 jnp.tile(diff_k_norm_g[0], n_heads2),
                                 jnp.ones((d,), F32)]).reshape(1, 3 * d)
    qkv = _qkv(x_all, t_all, t_lat, norm_mix_g[1:2], mod(1, 0), mod(1, 1), diff_w_qkv[0].astype(BF16), head_gain,
               cos_t, sin_t, ridx, seq, tm=tm_qkv)
    attn = _diff_attention(qkv, lam, diff_subln_g[0:1], batch, seq, ctx_len, d, 1.0 - lam_init)
    x_lat = _proj_residual(attn, diff_w_o[0].astype(BF16), x_all, mod(1, 2), ridx)
    x_lat = _moe(x_lat, t_lat, norm_ffn_g[1:2], mod(1, 3), mod(1, 4), mod(1, 5), ridx,
                 moe_w_router[1], moe_b_router[1], moe_w_gate_up[1], moe_b_gate_up[1],
                 moe_w_down[1], moe_b_down[1])
    return x_lat.reshape(batch, seq, d)
```

```python
import functools
import math

import jax
import jax.numpy as jnp
from jax import lax
from jax.experimental import pallas as pl
from jax.experimental.pallas import tpu as pltpu

F32 = jnp.float32
BF16 = jnp.bfloat16

NORM_EPS = 1e-6
N_MOD = 6
GRID_W = 64
GMLP_CHUNK = 128
GMLP_GROUPS = 8
DIFF_HEADS = 8
HEAD_DIM = 128
ROPE_THETA = 10000.0
N_EXPERTS = 32
TOP_K = 4
SWIGLU_ALPHA = 1.702
SWIGLU_LIMIT = 7.0

MOD_ROWS = 16
EXPERT_ROWS = 256
MIB = 1024 * 1024


def _cparams(vmem_mib):
    return pltpu.CompilerParams(vmem_limit_bytes=vmem_mib * MIB)


def _rms_mod(x, g, shift, scale):
    ms = jnp.mean(x * x, axis=-1, keepdims=True)
    y = x * lax.rsqrt(ms + NORM_EPS) * g
    return y * (1.0 + scale) + shift


def _mod_kernel(c_ref, w_ref, b_ref, o_ref):
    c = c_ref[...]
    s = c * jax.nn.sigmoid(c)
    o_ref[0] = jnp.dot(s.astype(BF16), w_ref[0].astype(BF16), preferred_element_type=F32) + b_ref[0]


def _modulation(c16, w_mod, b_mod):
    depth, d, n = w_mod.shape
    tn = 1024
    return pl.pallas_call(
        _mod_kernel,
        grid=(depth, n // tn),
        in_specs=[
            pl.BlockSpec((MOD_ROWS, d), lambda l, j: (0, 0)),
            pl.BlockSpec((1, d, tn), lambda l, j: (l, 0, j)),
            pl.BlockSpec((1, 1, tn), lambda l, j: (l, 0, j)),
        ],
        out_specs=pl.BlockSpec((1, MOD_ROWS, tn), lambda l, j: (l, 0, j)),
        out_shape=jax.ShapeDtypeStruct((depth, MOD_ROWS, n), F32),
        compiler_params=_cparams(40),
        name="modulation",
    )(c16, w_mod, b_mod.reshape(depth, 1, n))


def _gmlp_in_kernel(x_ref, g_ref, sh_ref, sc_ref, w_ref, o_ref, h_ref):
    @pl.when(pl.program_id(1) == 0)
    def _():
        h_ref[...] = _rms_mod(x_ref[...], g_ref[...], sh_ref[0], sc_ref[0]).astype(BF16)

    acc = jnp.dot(h_ref[...], w_ref[...], preferred_element_type=F32)
    gelu = 0.5 * acc * (1.0 + lax.erf(acc * (2.0 ** -0.5)))
    o_ref[...] = gelu.astype(o_ref.dtype)


def _gmlp_in(x, g, shift, scale, w_bf16, ridx, tm=1024, tn=1024):
    t, d = x.shape
    n = w_bf16.shape[1]
    return pl.pallas_call(
        _gmlp_in_kernel,
        grid=(t // tm, n // tn),
        in_specs=[
            pl.BlockSpec((tm, d), lambda i, j: (i, 0)),
            pl.BlockSpec((1, d), lambda i, j: (0, 0)),
            pl.BlockSpec((1, 1, d), lambda i, j: (ridx(i, tm), 0, 0)),
            pl.BlockSpec((1, 1, d), lambda i, j: (ridx(i, tm), 0, 0)),
            pl.BlockSpec((d, tn), lambda i, j: (0, j)),
        ],
        out_specs=pl.BlockSpec((tm, tn), lambda i, j: (i, j)),
        out_shape=jax.ShapeDtypeStruct((t, n), BF16),
        scratch_shapes=[pltpu.VMEM((tm, d), BF16)],
        compiler_params=_cparams(48),
        name="gmlp_in",
    )(x, g, shift, scale, w_bf16)


def _gmlp_out_kernel(u_ref, v_ref, lng_ref, ws_ref, bs_ref, wo_ref, x_ref, gate_ref, o_ref, t_ref):
    tm, width = v_ref.shape
    gw = width // GMLP_GROUPS
    v = v_ref[...].astype(F32)
    mu = jnp.mean(v, axis=-1, keepdims=True)
    vc = v - mu
    var = jnp.mean(vc * vc, axis=-1, keepdims=True)
    vn = (vc * lax.rsqrt(var + NORM_EPS) * lng_ref[...]).astype(BF16)
    for c in range(tm // GMLP_CHUNK):
        r0 = c * GMLP_CHUNK
        for g in range(GMLP_GROUPS):
            c0 = g * gw
            mixed = jnp.dot(ws_ref[g], vn[r0:r0 + GMLP_CHUNK, c0:c0 + gw], preferred_element_type=F32)
            mixed = mixed + bs_ref[:, g:g + 1]
            u = u_ref[r0:r0 + GMLP_CHUNK, c0:c0 + gw].astype(F32)
            t_ref[r0:r0 + GMLP_CHUNK, c0:c0 + gw] = (u * mixed).astype(BF16)
    y = jnp.dot(t_ref[...], wo_ref[...], preferred_element_type=F32)
    o_ref[...] = x_ref[...] + gate_ref[0] * y


def _gmlp_out(uv, ln_g, w_s_bf16, b_s_t, w_out_bf16, x, gate, ridx, tm=512):
    t, d = x.shape
    width = uv.shape[1] // 2
    return pl.pallas_call(
        _gmlp_out_kernel,
        grid=(t // tm,),
        in_specs=[
            pl.BlockSpec((tm, width), lambda i: (i, 0)),
            pl.BlockSpec((tm, width), lambda i: (i, 1)),
            pl.BlockSpec((1, width), lambda i: (0, 0)),
            pl.BlockSpec(w_s_bf16.shape, lambda i: (0, 0, 0)),
            pl.BlockSpec(b_s_t.shape, lambda i: (0, 0)),
            pl.BlockSpec((width, d), lambda i: (0, 0)),
            pl.BlockSpec((tm, d), lambda i: (i, 0)),
            pl.BlockSpec((1, 1, d), lambda i: (ridx(i, tm), 0, 0)),
        ],
        out_specs=pl.BlockSpec((tm, d), lambda i: (i, 0)),
        out_shape=jax.ShapeDtypeStruct((t, d), F32),
        scratch_shapes=[pltpu.VMEM((tm, width), BF16)],
        compiler_params=_cparams(56),
        name="gmlp_out",
    )(uv, uv, ln_g, w_s_bf16, b_s_t, w_out_bf16, x, gate)


def _qkv_kernel(x_ref, g_ref, sh_ref, sc_ref, w_ref, hg_ref, cos_ref, sin_ref, o_ref, h_ref, *, n_qk_tiles):
    j = pl.program_id(1)

    @pl.when(j == 0)
    def _():
        h_ref[...] = _rms_mod(x_ref[...], g_ref[...], sh_ref[0], sc_ref[0]).astype(BF16)

    tm, tn = o_ref.shape

    @pl.when(j < n_qk_tiles)
    def _():
        cos = cos_ref[...]
        sin = sin_ref[...]
        first_half = (lax.broadcasted_iota(jnp.int32, (tm, HEAD_DIM), 1) % (HEAD_DIM // 2)) < (HEAD_DIM // 4)
        chunk = 2 * HEAD_DIM
        for c in range(tn // chunk):
            acc = jnp.dot(h_ref[...], w_ref[:, c * chunk:(c + 1) * chunk], preferred_element_type=F32)
            for hh in range(chunk // HEAD_DIM):
                col = c * chunk + hh * HEAD_DIM
                a = acc[:, hh * HEAD_DIM:(hh + 1) * HEAD_DIM]
                ms = jnp.mean(a * a, axis=-1, keepdims=True)
                a = a * lax.rsqrt(ms + NORM_EPS) * hg_ref[:, col:col + HEAD_DIM]
                partner = jnp.where(first_half,
                                    pltpu.roll(a, HEAD_DIM - HEAD_DIM // 4, 1),
                                    pltpu.roll(a, HEAD_DIM // 4, 1))
                o_ref[:, col:col + HEAD_DIM] = (a * cos + partner * sin).astype(o_ref.dtype)

    @pl.when(j >= n_qk_tiles)
    def _():
        o_ref[...] = jnp.dot(h_ref[...], w_ref[...], preferred_element_type=F32).astype(o_ref.dtype)


def _qkv(x_all, t_rows, t_lat, g, shift, scale, w_bf16, head_gain, cos_t, sin_t, ridx, seq, tm=1024, tn=1024):
    d = x_all.shape[1]
    n = w_bf16.shape[1]
    n_qk_tiles = (2 * n // 3) // tn
    seq_blocks = seq // tm
    n_lat_blocks = t_lat // tm

    def rope_idx(i, j):
        return (jnp.where(i < n_lat_blocks, i % seq_blocks, seq_blocks), 0)

    return pl.pallas_call(
        functools.partial(_qkv_kernel, n_qk_tiles=n_qk_tiles),
        grid=(t_rows // tm, n // tn),
        in_specs=[
            pl.BlockSpec((tm, d), lambda i, j: (i, 0)),
            pl.BlockSpec((1, d), lambda i, j: (0, 0)),
            pl.BlockSpec((1, 1, d), lambda i, j: (ridx(i, tm), 0, 0)),
            pl.BlockSpec((1, 1, d), lambda i, j: (ridx(i, tm), 0, 0)),
            pl.BlockSpec((d, tn), lambda i, j: (0, j)),
            pl.BlockSpec((1, tn), lambda i, j: (0, j)),
            pl.BlockSpec((tm, HEAD_DIM), rope_idx),
            pl.BlockSpec((tm, HEAD_DIM), rope_idx),
        ],
        out_specs=pl.BlockSpec((tm, tn), lambda i, j: (i, j)),
        out_shape=jax.ShapeDtypeStruct((t_rows, n), BF16),
        scratch_shapes=[pltpu.VMEM((tm, d), BF16)],
        compiler_params=_cparams(48),
        name="qkv_proj",
    )(x_all, g, shift, scale, w_bf16, head_gain, cos_t, sin_t)


def _attn_kernel(lam_ref, q1_ref, q2_ref, k1l_ref, k2l_ref, k1c_ref, k2c_ref, vl_ref, vc_ref, sg_ref, o_ref,
                 *, out_scale):
    lam = lam_ref[0]
    nt = (((1,), (1,)), ((), ()))

    def softmax_times_v(q_ref, kc_ref, kl_ref):
        q = q_ref[...]
        sc = lax.dot_general(q, kc_ref[...], nt, preferred_element_type=F32)
        sl = lax.dot_general(q, kl_ref[...], nt, preferred_element_type=F32)
        m = jnp.maximum(jnp.max(sc, axis=-1, keepdims=True), jnp.max(sl, axis=-1, keepdims=True))
        pc = jnp.exp2(sc - m)
        pl_ = jnp.exp2(sl - m)
        z = jnp.sum(pc, axis=-1, keepdims=True) + jnp.sum(pl_, axis=-1, keepdims=True)
        pv = (jnp.dot(pc.astype(BF16), vc_ref[...], preferred_element_type=F32)
              + jnp.dot(pl_.astype(BF16), vl_ref[...], preferred_element_type=F32))
        return pv / z

    o = softmax_times_v(q1_ref, k1c_ref, k1l_ref) - lam * softmax_times_v(q2_ref, k2c_ref, k2l_ref)
    ms = jnp.mean(o * o, axis=-1, keepdims=True)
    o = o * lax.rsqrt(ms + NORM_EPS) * sg_ref[...] * out_scale
    o_ref[...] = o.astype(o_ref.dtype)


def _diff_attention(qkv, lam, subln_g, batch, seq, ctx_len, d, out_scale, tq=512):
    t_lat = batch * seq
    vd = 2 * HEAD_DIM
    qb = seq // tq
    k_col0 = d // HEAD_DIM
    v_col0 = 2 * d // vd
    ctx_row0 = t_lat // ctx_len

    return pl.pallas_call(
        functools.partial(_attn_kernel, out_scale=out_scale),
        grid=(batch, DIFF_HEADS, qb),
        in_specs=[
            pl.BlockSpec(memory_space=pltpu.SMEM),
            pl.BlockSpec((tq, HEAD_DIM), lambda b, h, q: (b * qb + q, 2 * h)),
            pl.BlockSpec((tq, HEAD_DIM), lambda b, h, q: (b * qb + q, 2 * h + 1)),
            pl.BlockSpec((seq, HEAD_DIM), lambda b, h, q: (b, k_col0 + 2 * h)),
            pl.BlockSpec((seq, HEAD_DIM), lambda b, h, q: (b, k_col0 + 2 * h + 1)),
            pl.BlockSpec((ctx_len, HEAD_DIM), lambda b, h, q: (ctx_row0 + b, k_col0 + 2 * h)),
            pl.BlockSpec((ctx_len, HEAD_DIM), lambda b, h, q: (ctx_row0 + b, k_col0 + 2 * h + 1)),
            pl.BlockSpec((seq, vd), lambda b, h, q: (b, v_col0 + h)),
            pl.BlockSpec((ctx_len, vd), lambda b, h, q: (ctx_row0 + b, v_col0 + h)),
            pl.BlockSpec((1, vd), lambda b, h, q: (0, 0)),
        ],
        out_specs=pl.BlockSpec((tq, vd), lambda b, h, q: (b * qb + q, h)),
        out_shape=jax.ShapeDtypeStruct((t_lat, d), BF16),
        compiler_params=_cparams(56),
        name="diff_attention",
    )(lam, qkv, qkv, qkv, qkv, qkv, qkv, qkv, qkv, subln_g)


def _proj_residual_kernel(a_ref, w_ref, x_ref, gate_ref, o_ref):
    y = jnp.dot(a_ref[...], w_ref[...], preferred_element_type=F32)
    o_ref[...] = x_ref[...] + gate_ref[0] * y


def _proj_residual(a, w_bf16, x_all, gate, ridx, tm=1024, tn=1024):
    t, k = a.shape
    n = w_bf16.shape[1]
    return pl.pallas_call(
        _proj_residual_kernel,
        grid=(t // tm, n // tn),
        in_specs=[
            pl.BlockSpec((tm, k), lambda i, j: (i, 0)),
            pl.BlockSpec((k, tn), lambda i, j: (0, j)),
            pl.BlockSpec((tm, tn), lambda i, j: (i, j)),
            pl.BlockSpec((1, 1, tn), lambda i, j: (ridx(i, tm), 0, j)),
        ],
        out_specs=pl.BlockSpec((tm, tn), lambda i, j: (i, j)),
        out_shape=jax.ShapeDtypeStruct((t, n), F32),
        compiler_params=_cparams(48),
        name="attn_out_proj",
    )(a, w_bf16, x_all, gate)


def _router_kernel(x_ref, g_ref, sh_ref, sc_ref, wr_ref, br_ref, h_ref, idx_ref, wt_ref):
    h = _rms_mod(x_ref[...], g_ref[...], sh_ref[0], sc_ref[0])
    h_ref[...] = h
    logits = lax.dot_general(wr_ref[...], h, (((1,), (1,)), ((), ())),
                             precision=lax.Precision.HIGHEST, preferred_element_type=F32) + br_ref[...]
    n_e, tm = logits.shape
    e_iota = lax.broadcasted_iota(jnp.int32, (n_e, tm), 0).astype(F32)
    vals, idxs = [], []
    for _ in range(TOP_K):
        m = jnp.max(logits, axis=0, keepdims=True)
        idx = jnp.min(jnp.where(logits == m, e_iota, float(n_e)), axis=0, keepdims=True)
        vals.append(m)
        idxs.append(idx)
        logits = jnp.where(e_iota == idx, -jnp.inf, logits)
    exps = [jnp.exp(v - vals[0]) for v in vals]
    denom = exps[0]
    for e in exps[1:]:
        denom = denom + e
    idx_ref[...] = jnp.zeros(idx_ref.shape, jnp.int32)
    wt_ref[...] = jnp.zeros(wt_ref.shape, F32)
    for k in range(TOP_K):
        idx_ref[k:k + 1, :] = idxs[k].astype(jnp.int32)
        wt_ref[k:k + 1, :] = exps[k] / denom


def _router(x_all, t_rows, g, shift, scale, w_r_t, b_r, ridx, tm=512):
    d = x_all.shape[1]
    n_e = w_r_t.shape[0]
    return pl.pallas_call(
        _router_kernel,
        grid=(t_rows // tm,),
        in_specs=[
            pl.BlockSpec((tm, d), lambda i: (i, 0)),
            pl.BlockSpec((1, d), lambda i: (0, 0)),
            pl.BlockSpec((1, 1, d), lambda i: (ridx(i, tm), 0, 0)),
            pl.BlockSpec((1, 1, d), lambda i: (ridx(i, tm), 0, 0)),
            pl.BlockSpec((n_e, d), lambda i: (0, 0)),
            pl.BlockSpec((n_e, 1), lambda i: (0, 0)),
        ],
        out_specs=[
            pl.BlockSpec((tm, d), lambda i: (i, 0)),
            pl.BlockSpec((8, tm), lambda i: (0, i)),
            pl.BlockSpec((8, tm), lambda i: (0, i)),
        ],
        out_shape=[
            jax.ShapeDtypeStruct((t_rows, d), F32),
            jax.ShapeDtypeStruct((8, t_rows), jnp.int32),
            jax.ShapeDtypeStruct((8, t_rows), F32),
        ],
        compiler_params=_cparams(40),
        name="moe_router",
    )(x_all, g, shift, scale, w_r_t, b_r)


PERM_TILE = 256


def _split_gate_up_kernel(w_ref, p_ref, og_ref, ol_ref):
    half = PERM_TILE // 2
    tn = w_ref.shape[2]
    for c in range(tn // PERM_TILE):
        w = w_ref[0, :, c * PERM_TILE:(c + 1) * PERM_TILE].astype(BF16)
        t = jnp.dot(w, p_ref[...], preferred_element_type=F32)
        og_ref[0, :, c * half:(c + 1) * half] = t[:, :half].astype(BF16)
        ol_ref[0, :, c * half:(c + 1) * half] = t[:, half:].astype(BF16)


def _split_gate_up(w_gu, tn=1024):
    n_e, d, n2 = w_gu.shape
    half = PERM_TILE // 2
    col = jnp.arange(PERM_TILE, dtype=jnp.int32)
    src = jnp.where(col < half, 2 * col, 2 * (col - half) + 1)
    perm = (jnp.arange(PERM_TILE, dtype=jnp.int32)[:, None] == src[None, :]).astype(BF16)
    out = jax.ShapeDtypeStruct((n_e, d, n2 // 2), BF16)
    return pl.pallas_call(
        _split_gate_up_kernel,
        grid=(n_e, n2 // tn),
        in_specs=[
            pl.BlockSpec((1, d, tn), lambda e, j: (e, 0, j)),
            pl.BlockSpec((PERM_TILE, PERM_TILE), lambda e, j: (0, 0)),
        ],
        out_specs=[
            pl.BlockSpec((1, d, tn // 2), lambda e, j: (e, 0, j)),
            pl.BlockSpec((1, d, tn // 2), lambda e, j: (e, 0, j)),
        ],
        out_shape=[out, out],
        compiler_params=_cparams(40),
        name="moe_split_gate_up",
    )(w_gu, perm)


def _expert_up_kernel(blk_e_ref, n_used_ref, tok_ref, tok_next_ref, h_ref, wg_ref, wl_ref, bg_ref, bl_ref, o_ref,
                      xbuf_ref, sem, *, nc):
    i = pl.program_id(0)
    n_used = n_used_ref[0]

    def row_copy(src_row, r, slot):
        return pltpu.make_async_copy(h_ref.at[pl.ds(src_row, 1)], xbuf_ref.at[slot, pl.ds(r, 1)], sem.at[slot])

    def start_gather(idx_ref, slot):
        for r in range(EXPERT_ROWS):
            row_copy(idx_ref[0, 0, r], r, slot).start()

    def wait_gather(slot):
        for r in range(EXPERT_ROWS):
            row_copy(0, r, slot).wait()

    @pl.when(i == 0)
    def _():
        start_gather(tok_ref, 0)

    @pl.when(i < n_used)
    def _():
        slot = lax.rem(i, 2)
        start_gather(tok_next_ref, 1 - slot)
        wait_gather(slot)
        xb = xbuf_ref[slot].astype(BF16)
        de = o_ref.shape[1]
        for c in range(de // nc):
            sl = slice(c * nc, (c + 1) * nc)
            gate = jnp.dot(xb, wg_ref[0, :, sl], preferred_element_type=F32) + bg_ref[0, :, sl]
            lin = jnp.dot(xb, wl_ref[0, :, sl], preferred_element_type=F32) + bl_ref[0, :, sl]
            glu = jnp.minimum(gate, SWIGLU_LIMIT)
            lin = jnp.clip(lin, -SWIGLU_LIMIT, SWIGLU_LIMIT)
            act = glu * jax.nn.sigmoid(SWIGLU_ALPHA * glu) * (lin + 1.0)
            o_ref[:, sl] = act.astype(o_ref.dtype)

        @pl.when(i + 1 >= n_used)
        def _():
            wait_gather(1 - slot)

    @pl.when(i >= n_used)
    def _():
        o_ref[...] = jnp.zeros(o_ref.shape, o_ref.dtype)


def _expert_up(h, tok_blocks, blk_e, n_used, wg, wl, bg, bl, nc=512):
    d = h.shape[1]
    de = wg.shape[2]
    n_blocks = tok_blocks.shape[0]

    def w_idx(i, be, nu):
        return (be[i], 0, 0)

    grid_spec = pltpu.PrefetchScalarGridSpec(
        num_scalar_prefetch=2,
        grid=(n_blocks,),
        in_specs=[
            pl.BlockSpec((1, 1, EXPERT_ROWS), lambda i, be, nu: (i, 0, 0), memory_space=pltpu.SMEM),
            pl.BlockSpec((1, 1, EXPERT_ROWS), lambda i, be, nu: (jnp.minimum(i + 1, n_blocks - 1), 0, 0),
                         memory_space=pltpu.SMEM),
            pl.BlockSpec(memory_space=pl.ANY),
            pl.BlockSpec((1, d, de), w_idx),
            pl.BlockSpec((1, d, de), w_idx),
            pl.BlockSpec((1, 1, de), w_idx),
            pl.BlockSpec((1, 1, de), w_idx),
        ],
        out_specs=pl.BlockSpec((EXPERT_ROWS, de), lambda i, be, nu: (i, 0)),
        scratch_shapes=[pltpu.VMEM((2, EXPERT_ROWS, d), h.dtype), pltpu.SemaphoreType.DMA((2,))],
    )
    return pl.pallas_call(
        functools.partial(_expert_up_kernel, nc=nc),
        grid_spec=grid_spec,
        out_shape=jax.ShapeDtypeStruct((n_blocks * EXPERT_ROWS, de), BF16),
        compiler_params=_cparams(56),
        name="moe_expert_up",
    )(blk_e, n_used, tok_blocks, tok_blocks, h, wg, wl, bg, bl)


def _expert_down_kernel(blk_e_ref, n_used_ref, a_ref, wd_ref, bd_ref, o_ref, wb_ref):
    i = pl.program_id(0)
    new_expert = jnp.logical_or(i == 0, blk_e_ref[i] != blk_e_ref[jnp.maximum(i - 1, 0)])

    @pl.when(jnp.logical_and(i < n_used_ref[0], new_expert))
    def _():
        wb_ref[...] = wd_ref[0].astype(BF16)

    @pl.when(i < n_used_ref[0])
    def _():
        o_ref[...] = jnp.dot(a_ref[...], wb_ref[...], preferred_element_type=F32) + bd_ref[0]

    @pl.when(i >= n_used_ref[0])
    def _():
        o_ref[...] = jnp.zeros(o_ref.shape, o_ref.dtype)


def _expert_down(act, blk_e, n_used, wd, bd):
    n_pad, de = act.shape
    d = wd.shape[2]
    n_blocks = n_pad // EXPERT_ROWS

    def row_idx(i, be, nu):
        return (jnp.minimum(i, nu[0] - 1), 0)

    def w_idx(i, be, nu):
        return (be[i], 0, 0)

    grid_spec = pltpu.PrefetchScalarGridSpec(
        num_scalar_prefetch=2,
        grid=(n_blocks,),
        in_specs=[
            pl.BlockSpec((EXPERT_ROWS, de), row_idx),
            pl.BlockSpec((1, de, d), w_idx),
            pl.BlockSpec((1, 1, d), w_idx),
        ],
        out_specs=pl.BlockSpec((EXPERT_ROWS, d), lambda i, be, nu: (i, 0)),
        scratch_shapes=[pltpu.VMEM((de, d), BF16)],
    )
    return pl.pallas_call(
        _expert_down_kernel,
        grid_spec=grid_spec,
        out_shape=jax.ShapeDtypeStruct((n_pad, d), F32),
        compiler_params=_cparams(56),
        name="moe_expert_down",
    )(blk_e, n_used, act, wd, bd)


def _combine_kernel(dest_ref, dest_next_ref, wt_ref, x_ref, gate_ref, ys_ref, o_ref, buf_ref, sem):
    i = pl.program_id(0)
    tm = x_ref.shape[0]

    def row_copy(src_row, k, r, slot):
        return pltpu.make_async_copy(ys_ref.at[pl.ds(src_row, 1)], buf_ref.at[slot, k, pl.ds(r, 1)], sem.at[slot])

    def start_gather(idx_ref, slot):
        def issue(r, carry):
            for k in range(TOP_K):
                row_copy(idx_ref[0, 0, k * tm + r], k, r, slot).start()
            return carry

        lax.fori_loop(0, tm, issue, 0, unroll=8)

    def wait_gather(slot):
        def drain(r, carry):
            for k in range(TOP_K):
                row_copy(0, k, r, slot).wait()
            return carry

        lax.fori_loop(0, tm, drain, 0, unroll=8)

    slot = lax.rem(i, 2)

    @pl.when(i == 0)
    def _():
        start_gather(dest_ref, 0)

    @pl.when(i + 1 < pl.num_programs(0))
    def _():
        start_gather(dest_next_ref, 1 - slot)

    wait_gather(slot)
    acc = wt_ref[:, 0:1] * buf_ref[slot, 0]
    for k in range(1, TOP_K):
        acc = acc + wt_ref[:, k:k + 1] * buf_ref[slot, k]
    o_ref[...] = x_ref[...] + gate_ref[0] * acc


def _combine(ys, dest_blocks, wt, x_all, t_rows, gate, ridx, tm):
    d = x_all.shape[1]
    n_steps = t_rows // tm
    return pl.pallas_call(
        _combine_kernel,
        grid=(n_steps,),
        in_specs=[
            pl.BlockSpec((1, 1, TOP_K * tm), lambda i: (i, 0, 0), memory_space=pltpu.SMEM),
            pl.BlockSpec((1, 1, TOP_K * tm), lambda i: (jnp.minimum(i + 1, n_steps - 1), 0, 0),
                         memory_space=pltpu.SMEM),
            pl.BlockSpec((tm, TOP_K), lambda i: (i, 0)),
            pl.BlockSpec((tm, d), lambda i: (i, 0)),
            pl.BlockSpec((1, 1, d), lambda i: (ridx(i, tm), 0, 0)),
            pl.BlockSpec(memory_space=pl.ANY),
        ],
        out_specs=pl.BlockSpec((tm, d), lambda i: (i, 0)),
        out_shape=jax.ShapeDtypeStruct((t_rows, d), F32),
        scratch_shapes=[pltpu.VMEM((2, TOP_K, tm, d), F32), pltpu.SemaphoreType.DMA((2,))],
        compiler_params=_cparams(32),
        name="moe_combine",
    )(dest_blocks, dest_blocks, wt, x_all, gate, ys)


def _slot_blocks(dest, tm):
    k, t = dest.shape
    return dest.reshape(k, t // tm, tm).transpose(1, 0, 2).reshape(t // tm, 1, k * tm)


def _moe(x_all, t_rows, norm_g, shift, scale, gate, ridx, w_r, b_r, layer, wg, wl, bg, bl, wd, bd, combine_tm=128):
    n_e = w_r.shape[1]
    h, top_idx, top_w = _router(x_all, t_rows, norm_g, shift, scale, w_r.T, b_r.reshape(n_e, 1), ridx)
    top_idx = top_idx[:TOP_K]
    top_w = top_w[:TOP_K]

    n_slots = TOP_K * t_rows
    e_flat = top_idx.reshape(n_slots)
    onehot = (e_flat[:, None] == jnp.arange(n_e, dtype=jnp.int32)[None, :]).astype(jnp.int32)
    csum = jnp.cumsum(onehot, axis=0)
    rank = jnp.sum(onehot * csum, axis=1) - 1
    counts = csum[-1]
    padded = (counts + EXPERT_ROWS - 1) // EXPERT_ROWS * EXPERT_ROWS
    pad_ends = jnp.cumsum(padded)
    pad_starts = pad_ends - padded
    dest = (jnp.sum(onehot * pad_starts[None, :], axis=1) + rank).reshape(TOP_K, t_rows)
    n_blocks = -(-(n_slots + n_e * (EXPERT_ROWS - 1)) // EXPERT_ROWS)
    n_pad = n_blocks * EXPERT_ROWS
    blk_row0 = jnp.arange(n_blocks, dtype=jnp.int32) * EXPERT_ROWS
    blk_e = jnp.minimum(jnp.sum((pad_ends[None, :] <= blk_row0[:, None]).astype(jnp.int32), axis=1), n_e - 1)
    n_used = (pad_ends[-1:] // EXPERT_ROWS).astype(jnp.int32)
    tok_of_slot = jnp.tile(jnp.arange(t_rows, dtype=jnp.int32), TOP_K)
    tok_blocks = jnp.zeros((n_pad,), jnp.int32).at[dest.reshape(n_slots)].set(
        tok_of_slot, unique_indices=True).reshape(n_blocks, 1, EXPERT_ROWS)
    blk_w = (blk_e + layer * n_e).astype(jnp.int32)

    act = _expert_up(h, tok_blocks, blk_w, n_used, wg, wl, bg, bl)
    ys = _expert_down(act, blk_w, n_used, wd, bd)
    return _combine(ys, _slot_blocks(dest, combine_tm), top_w.T, x_all, t_rows, gate, ridx, combine_tm)


def _rope_tables(seq, pad_rows):
    rows = seq // GRID_W
    row_pos = jnp.repeat(jnp.arange(rows), GRID_W).astype(F32)
    col_pos = jnp.tile(jnp.arange(GRID_W), rows).astype(F32)
    n_freq = HEAD_DIM // 4
    inv_freq = ROPE_THETA ** (-jnp.arange(n_freq, dtype=F32) / n_freq)
    ang_r = row_pos[:, None] * inv_freq
    ang_c = col_pos[:, None] * inv_freq
    cos = jnp.concatenate([jnp.cos(ang_r)] * 2 + [jnp.cos(ang_c)] * 2, axis=1)
    sin = jnp.concatenate([-jnp.sin(ang_r), jnp.sin(ang_r), -jnp.sin(ang_c), jnp.sin(ang_c)], axis=1)
    cos = jnp.concatenate([cos, jnp.ones((pad_rows, HEAD_DIM), F32)], axis=0)
    sin = jnp.concatenate([sin, jnp.zeros((pad_rows, HEAD_DIM), F32)], axis=0)
    return cos, sin


def kernel(x, c, ctx, c_ctx, w_mod, b_mod, norm_mix_g, norm_ffn_g, gmlp_w_in, gmlp_norm_g, gmlp_w_s, gmlp_b_s,
           gmlp_w_out, diff_w_qkv, diff_q_norm_g, diff_k_norm_g, diff_lambda, diff_subln_g, diff_w_o,
           moe_w_router, moe_b_router, moe_w_gate_up, moe_b_gate_up, moe_w_down, moe_b_down):
    batch, seq, d = x.shape
    ctx_len = ctx.shape[1]
    depth = w_mod.shape[0]
    t_lat = batch * seq
    t_ctx = batch * ctx_len
    t_all = t_lat + t_ctx
    assert batch < MOD_ROWS and depth == 2

    def ridx(i, tm):
        return jnp.minimum((i * tm) // seq, batch)

    c16 = jnp.concatenate([c, c_ctx[None, :], jnp.zeros((MOD_ROWS - batch - 1, d), F32)], axis=0)
    mods = _modulation(c16, w_mod, b_mod).reshape(depth, MOD_ROWS, N_MOD, d)

    def mod(layer, k):
        return mods[layer, :, k, :].reshape(MOD_ROWS, 1, d)

    x_all = jnp.concatenate([x.reshape(t_lat, d), ctx.reshape(t_ctx, d)], axis=0)

    n_e, de2 = moe_w_gate_up.shape[1], moe_w_gate_up.shape[3]
    de = de2 // 2
    wg, wl = _split_gate_up(moe_w_gate_up.reshape(depth * n_e, d, de2))
    bg = moe_b_gate_up[:, :, 0::2].reshape(depth * n_e, 1, de)
    bl = moe_b_gate_up[:, :, 1::2].reshape(depth * n_e, 1, de)
    wd = moe_w_down.reshape(depth * n_e, de, d)
    bd = moe_b_down.reshape(depth * n_e, 1, d)

    uv = _gmlp_in(x_all, norm_mix_g[0:1], mod(0, 0), mod(0, 1), gmlp_w_in[0].astype(BF16), ridx)
    x_all = _gmlp_out(uv, gmlp_norm_g[0:1], gmlp_w_s[0].astype(BF16), gmlp_b_s[0].T,
                      gmlp_w_out[0].astype(BF16), x_all, mod(0, 2), ridx)
    x_all = _moe(x_all, t_all, norm_ffn_g[0:1], mod(0, 3), mod(0, 4), mod(0, 5), ridx,
                 moe_w_router[0], moe_b_router[0], 0, wg, wl, bg, bl, wd, bd)

    lam_init = 0.8 - 0.6 * math.exp(-0.3 * 1)
    lf = diff_lambda[0].astype(F32)
    lam = (jnp.exp(jnp.sum(lf[0] * lf[1])) - jnp.exp(jnp.sum(lf[2] * lf[3])) + lam_init).reshape(1)
    tm_qkv = 1024
    cos_t, sin_t = _rope_tables(seq, tm_qkv)
    n_heads2 = d // HEAD_DIM
    q_scale = HEAD_DIM ** -0.5 * math.log2(math.e)
    head_gain = jnp.concatenate([jnp.tile(diff_q_norm_g[0] * q_scale, n_heads2),
                                 jnp.tile(diff_k_norm_g[0], n_heads2),
                                 jnp.ones((d,), F32)]).reshape(1, 3 * d)
    qkv = _qkv(x_all, t_all, t_lat, norm_mix_g[1:2], mod(1, 0), mod(1, 1), diff_w_qkv[0].astype(BF16), head_gain,
               cos_t, sin_t, ridx, seq, tm=tm_qkv)
    attn = _diff_attention(qkv, lam, diff_subln_g[0:1], batch, seq, ctx_len, d, 1.0 - lam_init)
    x_lat = _proj_residual(attn, diff_w_o[0].astype(BF16), x_all, mod(1, 2), ridx)
    x_lat = _moe(x_lat, t_lat, norm_ffn_g[1:2], mod(1, 3), mod(1, 4), mod(1, 5), ridx,
                 moe_w_router[1], moe_b_router[1], 1, wg, wl, bg, bl, wd, bd)
    return x_lat.reshape(batch, seq, d)
```

```python
import functools
import math

import jax
import jax.numpy as jnp
from jax import lax
from jax.experimental import pallas as pl
from jax.experimental.pallas import tpu as pltpu

F32 = jnp.float32
BF16 = jnp.bfloat16

NORM_EPS = 1e-6
N_MOD = 6
GRID_W = 64
GMLP_CHUNK = 128
GMLP_GROUPS = 8
DIFF_HEADS = 8
HEAD_DIM = 128
ROPE_THETA = 10000.0
N_EXPERTS = 32
TOP_K = 4
SWIGLU_ALPHA = 1.702
SWIGLU_LIMIT = 7.0

MOD_ROWS = 16
EXPERT_ROWS = 256
MIB = 1024 * 1024
LANES = 128


def _cparams(vmem_mib):
    return pltpu.CompilerParams(vmem_limit_bytes=vmem_mib * MIB)


def _rms_mod(x, g, shift, scale):
    ms = jnp.mean(x * x, axis=-1, keepdims=True)
    y = x * lax.rsqrt(ms + NORM_EPS) * g
    return y * (1.0 + scale) + shift


def _store_row_slabs(ref, value):
    rows, d = value.shape
    n = d // LANES
    for s in range(n):
        ref[pl.ds(s, rows, stride=n), :] = value[:, s * LANES:(s + 1) * LANES]


def _load_row_slabs(ref, rows, n, dtype):
    return jnp.concatenate([ref[pl.ds(s, rows, stride=n), :].astype(dtype) for s in range(n)], axis=1)


def _mod_kernel(c_ref, w_ref, b_ref, o_ref):
    c = c_ref[...]
    s = c * jax.nn.sigmoid(c)
    o_ref[0] = jnp.dot(s.astype(BF16), w_ref[0].astype(BF16), preferred_element_type=F32) + b_ref[0]


def _modulation(c16, w_mod, b_mod):
    depth, d, n = w_mod.shape
    tn = 1024
    return pl.pallas_call(
        _mod_kernel,
        grid=(depth, n // tn),
        in_specs=[
            pl.BlockSpec((MOD_ROWS, d), lambda l, j: (0, 0)),
            pl.BlockSpec((1, d, tn), lambda l, j: (l, 0, j)),
            pl.BlockSpec((1, 1, tn), lambda l, j: (l, 0, j)),
        ],
        out_specs=pl.BlockSpec((1, MOD_ROWS, tn), lambda l, j: (l, 0, j)),
        out_shape=jax.ShapeDtypeStruct((depth, MOD_ROWS, n), F32),
        compiler_params=_cparams(40),
        name="modulation",
    )(c16, w_mod, b_mod.reshape(depth, 1, n))


def _gmlp_in_kernel(x_ref, g_ref, sh_ref, sc_ref, w_ref, o_ref, h_ref):
    @pl.when(pl.program_id(1) == 0)
    def _():
        h_ref[...] = _rms_mod(x_ref[...], g_ref[...], sh_ref[0], sc_ref[0]).astype(BF16)

    acc = jnp.dot(h_ref[...], w_ref[...], preferred_element_type=F32)
    gelu = 0.5 * acc * (1.0 + lax.erf(acc * (2.0 ** -0.5)))
    o_ref[...] = gelu.astype(o_ref.dtype)


def _gmlp_in(x, g, shift, scale, w_bf16, ridx, tm=1024, tn=1024):
    t, d = x.shape
    n = w_bf16.shape[1]
    return pl.pallas_call(
        _gmlp_in_kernel,
        grid=(t // tm, n // tn),
        in_specs=[
            pl.BlockSpec((tm, d), lambda i, j: (i, 0)),
            pl.BlockSpec((1, d), lambda i, j: (0, 0)),
            pl.BlockSpec((1, 1, d), lambda i, j: (ridx(i, tm), 0, 0)),
            pl.BlockSpec((1, 1, d), lambda i, j: (ridx(i, tm), 0, 0)),
            pl.BlockSpec((d, tn), lambda i, j: (0, j)),
        ],
        out_specs=pl.BlockSpec((tm, tn), lambda i, j: (i, j)),
        out_shape=jax.ShapeDtypeStruct((t, n), BF16),
        scratch_shapes=[pltpu.VMEM((tm, d), BF16)],
        compiler_params=_cparams(48),
        name="gmlp_in",
    )(x, g, shift, scale, w_bf16)


def _gmlp_out_kernel(u_ref, v_ref, lng_ref, ws_ref, bs_ref, wo_ref, x_ref, gate_ref, o_ref, t_ref):
    tm, width = v_ref.shape
    gw = width // GMLP_GROUPS
    v = v_ref[...].astype(F32)
    mu = jnp.mean(v, axis=-1, keepdims=True)
    vc = v - mu
    var = jnp.mean(vc * vc, axis=-1, keepdims=True)
    vn = (vc * lax.rsqrt(var + NORM_EPS) * lng_ref[...]).astype(BF16)
    for c in range(tm // GMLP_CHUNK):
        r0 = c * GMLP_CHUNK
        for g in range(GMLP_GROUPS):
            c0 = g * gw
            mixed = jnp.dot(ws_ref[g], vn[r0:r0 + GMLP_CHUNK, c0:c0 + gw], preferred_element_type=F32)
            mixed = mixed + bs_ref[:, g:g + 1]
            u = u_ref[r0:r0 + GMLP_CHUNK, c0:c0 + gw].astype(F32)
            t_ref[r0:r0 + GMLP_CHUNK, c0:c0 + gw] = (u * mixed).astype(BF16)
    y = jnp.dot(t_ref[...], wo_ref[...], preferred_element_type=F32)
    o_ref[...] = x_ref[...] + gate_ref[0] * y


def _gmlp_out(uv, ln_g, w_s_bf16, b_s_t, w_out_bf16, x, gate, ridx, tm=512):
    t, d = x.shape
    width = uv.shape[1] // 2
    return pl.pallas_call(
        _gmlp_out_kernel,
        grid=(t // tm,),
        in_specs=[
            pl.BlockSpec((tm, width), lambda i: (i, 0)),
            pl.BlockSpec((tm, width), lambda i: (i, 1)),
            pl.BlockSpec((1, width), lambda i: (0, 0)),
            pl.BlockSpec(w_s_bf16.shape, lambda i: (0, 0, 0)),
            pl.BlockSpec(b_s_t.shape, lambda i: (0, 0)),
            pl.BlockSpec((width, d), lambda i: (0, 0)),
            pl.BlockSpec((tm, d), lambda i: (i, 0)),
            pl.BlockSpec((1, 1, d), lambda i: (ridx(i, tm), 0, 0)),
        ],
        out_specs=pl.BlockSpec((tm, d), lambda i: (i, 0)),
        out_shape=jax.ShapeDtypeStruct((t, d), F32),
        scratch_shapes=[pltpu.VMEM((tm, width), BF16)],
        compiler_params=_cparams(56),
        name="gmlp_out",
    )(uv, uv, ln_g, w_s_bf16, b_s_t, w_out_bf16, x, gate)


def _qkv_kernel(x_ref, g_ref, sh_ref, sc_ref, w_ref, hg_ref, cos_ref, sin_ref, o_ref, h_ref, *, n_qk_tiles):
    j = pl.program_id(1)

    @pl.when(j == 0)
    def _():
        h_ref[...] = _rms_mod(x_ref[...], g_ref[...], sh_ref[0], sc_ref[0]).astype(BF16)

    acc = jnp.dot(h_ref[...], w_ref[...], preferred_element_type=F32)
    tm, tn = acc.shape

    @pl.when(j < n_qk_tiles)
    def _():
        cos = cos_ref[...]
        sin = sin_ref[...]
        first_half = (lax.broadcasted_iota(jnp.int32, (tm, HEAD_DIM), 1) % (HEAD_DIM // 2)) < (HEAD_DIM // 4)
        for hh in range(tn // HEAD_DIM):
            a = acc[:, hh * HEAD_DIM:(hh + 1) * HEAD_DIM]
            ms = jnp.mean(a * a, axis=-1, keepdims=True)
            a = a * lax.rsqrt(ms + NORM_EPS) * hg_ref[:, hh * HEAD_DIM:(hh + 1) * HEAD_DIM]
            partner = jnp.where(first_half,
                                pltpu.roll(a, HEAD_DIM - HEAD_DIM // 4, 1),
                                pltpu.roll(a, HEAD_DIM // 4, 1))
            o_ref[:, hh * HEAD_DIM:(hh + 1) * HEAD_DIM] = (a * cos + partner * sin).astype(o_ref.dtype)

    @pl.when(j >= n_qk_tiles)
    def _():
        o_ref[...] = acc.astype(o_ref.dtype)


def _qkv(x_all, t_rows, t_lat, g, shift, scale, w_bf16, head_gain, cos_t, sin_t, ridx, seq, tm=1024, tn=1024):
    d = x_all.shape[1]
    n = w_bf16.shape[1]
    n_qk_tiles = (2 * n // 3) // tn
    seq_blocks = seq // tm
    n_lat_blocks = t_lat // tm

    def rope_idx(i, j):
        return (jnp.where(i < n_lat_blocks, i % seq_blocks, seq_blocks), 0)

    return pl.pallas_call(
        functools.partial(_qkv_kernel, n_qk_tiles=n_qk_tiles),
        grid=(t_rows // tm, n // tn),
        in_specs=[
            pl.BlockSpec((tm, d), lambda i, j: (i, 0)),
            pl.BlockSpec((1, d), lambda i, j: (0, 0)),
            pl.BlockSpec((1, 1, d), lambda i, j: (ridx(i, tm), 0, 0)),
            pl.BlockSpec((1, 1, d), lambda i, j: (ridx(i, tm), 0, 0)),
            pl.BlockSpec((d, tn), lambda i, j: (0, j)),
            pl.BlockSpec((1, tn), lambda i, j: (0, j)),
            pl.BlockSpec((tm, HEAD_DIM), rope_idx),
            pl.BlockSpec((tm, HEAD_DIM), rope_idx),
        ],
        out_specs=pl.BlockSpec((tm, tn), lambda i, j: (i, j)),
        out_shape=jax.ShapeDtypeStruct((t_rows, n), BF16),
        scratch_shapes=[pltpu.VMEM((tm, d), BF16)],
        compiler_params=_cparams(48),
        name="qkv_proj",
    )(x_all, g, shift, scale, w_bf16, head_gain, cos_t, sin_t)


def _attn_kernel(lam_ref, q1_ref, q2_ref, k1l_ref, k2l_ref, k1c_ref, k2c_ref, vl_ref, vc_ref, sg_ref, o_ref,
                 *, out_scale, key_chunk):
    lam = lam_ref[0]
    nt = (((1,), (1,)), ((), ()))

    def softmax_times_v(q_ref, kc_ref, kl_ref):
        q = q_ref[...]
        tq = q.shape[0]
        chunks = [(kc_ref, vc_ref, c0, min(key_chunk, kc_ref.shape[0] - c0))
                  for c0 in range(0, kc_ref.shape[0], key_chunk)]
        chunks += [(kl_ref, vl_ref, c0, min(key_chunk, kl_ref.shape[0] - c0))
                   for c0 in range(0, kl_ref.shape[0], key_chunk)]
        m = jnp.full((tq, 1), -jnp.inf, F32)
        z = jnp.zeros((tq, 1), F32)
        acc = jnp.zeros((tq, vl_ref.shape[1]), F32)
        for k_ref, v_ref, c0, n in chunks:
            s = lax.dot_general(q, k_ref[c0:c0 + n, :], nt, preferred_element_type=F32)
            m_new = jnp.maximum(m, jnp.max(s, axis=-1, keepdims=True))
            alpha = jnp.exp2(m - m_new)
            p = jnp.exp2(s - m_new)
            z = alpha * z + jnp.sum(p, axis=-1, keepdims=True)
            acc = alpha * acc + jnp.dot(p.astype(BF16), v_ref[c0:c0 + n, :], preferred_element_type=F32)
            m = m_new
        return acc / z

    o = softmax_times_v(q1_ref, k1c_ref, k1l_ref) - lam * softmax_times_v(q2_ref, k2c_ref, k2l_ref)
    ms = jnp.mean(o * o, axis=-1, keepdims=True)
    o = o * lax.rsqrt(ms + NORM_EPS) * sg_ref[...] * out_scale
    o_ref[...] = o.astype(o_ref.dtype)


def _diff_attention(qkv, lam, subln_g, batch, seq, ctx_len, d, out_scale, tq=512, key_chunk=512):
    t_lat = batch * seq
    vd = 2 * HEAD_DIM
    qb = seq // tq
    k_col0 = d // HEAD_DIM
    v_col0 = 2 * d // vd
    ctx_row0 = t_lat // ctx_len

    return pl.pallas_call(
        functools.partial(_attn_kernel, out_scale=out_scale, key_chunk=key_chunk),
        grid=(batch, DIFF_HEADS, qb),
        in_specs=[
            pl.BlockSpec(memory_space=pltpu.SMEM),
            pl.BlockSpec((tq, HEAD_DIM), lambda b, h, q: (b * qb + q, 2 * h)),
            pl.BlockSpec((tq, HEAD_DIM), lambda b, h, q: (b * qb + q, 2 * h + 1)),
            pl.BlockSpec((seq, HEAD_DIM), lambda b, h, q: (b, k_col0 + 2 * h)),
            pl.BlockSpec((seq, HEAD_DIM), lambda b, h, q: (b, k_col0 + 2 * h + 1)),
            pl.BlockSpec((ctx_len, HEAD_DIM), lambda b, h, q: (ctx_row0 + b, k_col0 + 2 * h)),
            pl.BlockSpec((ctx_len, HEAD_DIM), lambda b, h, q: (ctx_row0 + b, k_col0 + 2 * h + 1)),
            pl.BlockSpec((seq, vd), lambda b, h, q: (b, v_col0 + h)),
            pl.BlockSpec((ctx_len, vd), lambda b, h, q: (ctx_row0 + b, v_col0 + h)),
            pl.BlockSpec((1, vd), lambda b, h, q: (0, 0)),
        ],
        out_specs=pl.BlockSpec((tq, vd), lambda b, h, q: (b * qb + q, h)),
        out_shape=jax.ShapeDtypeStruct((t_lat, d), BF16),
        compiler_params=_cparams(56),
        name="diff_attention",
    )(lam, qkv, qkv, qkv, qkv, qkv, qkv, qkv, qkv, subln_g)


def _proj_residual_kernel(a_ref, w_ref, x_ref, gate_ref, o_ref):
    y = jnp.dot(a_ref[...], w_ref[...], preferred_element_type=F32)
    o_ref[...] = x_ref[...] + gate_ref[0] * y


def _proj_residual(a, w_bf16, x_all, gate, ridx, tm=1024, tn=1024):
    t, k = a.shape
    n = w_bf16.shape[1]
    return pl.pallas_call(
        _proj_residual_kernel,
        grid=(t // tm, n // tn),
        in_specs=[
            pl.BlockSpec((tm, k), lambda i, j: (i, 0)),
            pl.BlockSpec((k, tn), lambda i, j: (0, j)),
            pl.BlockSpec((tm, tn), lambda i, j: (i, j)),
            pl.BlockSpec((1, 1, tn), lambda i, j: (ridx(i, tm), 0, j)),
        ],
        out_specs=pl.BlockSpec((tm, tn), lambda i, j: (i, j)),
        out_shape=jax.ShapeDtypeStruct((t, n), F32),
        compiler_params=_cparams(48),
        name="attn_out_proj",
    )(a, w_bf16, x_all, gate)


def _router_kernel(x_ref, g_ref, sh_ref, sc_ref, wr_ref, br_ref, h_ref, idx_ref, wt_ref):
    h = _rms_mod(x_ref[...], g_ref[...], sh_ref[0], sc_ref[0])
    _store_row_slabs(h_ref, h)
    logits = lax.dot_general(wr_ref[...], h, (((1,), (1,)), ((), ())),
                             precision=lax.Precision.HIGHEST, preferred_element_type=F32) + br_ref[...]
    n_e, tm = logits.shape
    e_iota = lax.broadcasted_iota(jnp.int32, (n_e, tm), 0).astype(F32)
    vals, idxs = [], []
    for _ in range(TOP_K):
        m = jnp.max(logits, axis=0, keepdims=True)
        idx = jnp.min(jnp.where(logits == m, e_iota, float(n_e)), axis=0, keepdims=True)
        vals.append(m)
        idxs.append(idx)
        logits = jnp.where(e_iota == idx, -jnp.inf, logits)
    exps = [jnp.exp(v - vals[0]) for v in vals]
    denom = exps[0]
    for e in exps[1:]:
        denom = denom + e
    idx_ref[...] = jnp.zeros(idx_ref.shape, jnp.int32)
    wt_ref[...] = jnp.zeros(wt_ref.shape, F32)
    for k in range(TOP_K):
        idx_ref[k:k + 1, :] = idxs[k].astype(jnp.int32)
        wt_ref[k:k + 1, :] = exps[k] / denom


def _router(x_all, t_rows, g, shift, scale, w_r_t, b_r, ridx, tm=512):
    d = x_all.shape[1]
    n_e = w_r_t.shape[0]
    return pl.pallas_call(
        _router_kernel,
        grid=(t_rows // tm,),
        in_specs=[
            pl.BlockSpec((tm, d), lambda i: (i, 0)),
            pl.BlockSpec((1, d), lambda i: (0, 0)),
            pl.BlockSpec((1, 1, d), lambda i: (ridx(i, tm), 0, 0)),
            pl.BlockSpec((1, 1, d), lambda i: (ridx(i, tm), 0, 0)),
            pl.BlockSpec((n_e, d), lambda i: (0, 0)),
            pl.BlockSpec((n_e, 1), lambda i: (0, 0)),
        ],
        out_specs=[
            pl.BlockSpec((tm * (d // LANES), LANES), lambda i: (i, 0)),
            pl.BlockSpec((8, tm), lambda i: (0, i)),
            pl.BlockSpec((8, tm), lambda i: (0, i)),
        ],
        out_shape=[
            jax.ShapeDtypeStruct((t_rows * (d // LANES), LANES), F32),
            jax.ShapeDtypeStruct((8, t_rows), jnp.int32),
            jax.ShapeDtypeStruct((8, t_rows), F32),
        ],
        compiler_params=_cparams(40),
        name="moe_router",
    )(x_all, g, shift, scale, w_r_t, b_r)


PERM_TILE = 256


def _split_gate_up_kernel(w_ref, p_ref, og_ref, ol_ref):
    half = PERM_TILE // 2
    tn = w_ref.shape[2]
    for c in range(tn // PERM_TILE):
        w = w_ref[0, :, c * PERM_TILE:(c + 1) * PERM_TILE].astype(BF16)
        t = jnp.dot(w, p_ref[...], preferred_element_type=F32)
        og_ref[0, :, c * half:(c + 1) * half] = t[:, :half].astype(BF16)
        ol_ref[0, :, c * half:(c + 1) * half] = t[:, half:].astype(BF16)


def _split_gate_up(w_gu, tn=1024):
    n_e, d, n2 = w_gu.shape
    half = PERM_TILE // 2
    col = jnp.arange(PERM_TILE, dtype=jnp.int32)
    src = jnp.where(col < half, 2 * col, 2 * (col - half) + 1)
    perm = (jnp.arange(PERM_TILE, dtype=jnp.int32)[:, None] == src[None, :]).astype(BF16)
    out = jax.ShapeDtypeStruct((n_e, d, n2 // 2), BF16)
    return pl.pallas_call(
        _split_gate_up_kernel,
        grid=(n_e, n2 // tn),
        in_specs=[
            pl.BlockSpec((1, d, tn), lambda e, j: (e, 0, j)),
            pl.BlockSpec((PERM_TILE, PERM_TILE), lambda e, j: (0, 0)),
        ],
        out_specs=[
            pl.BlockSpec((1, d, tn // 2), lambda e, j: (e, 0, j)),
            pl.BlockSpec((1, d, tn // 2), lambda e, j: (e, 0, j)),
        ],
        out_shape=[out, out],
        compiler_params=_cparams(40),
        name="moe_split_gate_up",
    )(w_gu, perm)


def _expert_up_kernel(blk_e_ref, n_used_ref, tok_ref, tok_next_ref, h_ref, wg_ref, wl_ref, bg_ref, bl_ref, o_ref,
                      xbuf_ref, sem, *, nc):
    i = pl.program_id(0)
    n_used = n_used_ref[0]

    n_slab = xbuf_ref.shape[1] // EXPERT_ROWS

    def row_copy(src_row, r, slot):
        src = pl.multiple_of(src_row * n_slab, n_slab)
        return pltpu.make_async_copy(h_ref.at[pl.ds(src, n_slab)], xbuf_ref.at[slot, pl.ds(r * n_slab, n_slab)],
                                     sem.at[slot])

    def start_gather(idx_ref, slot):
        for r in range(EXPERT_ROWS):
            row_copy(idx_ref[0, 0, r], r, slot).start()

    def wait_gather(slot):
        for r in range(EXPERT_ROWS):
            row_copy(0, r, slot).wait()

    @pl.when(i == 0)
    def _():
        start_gather(tok_ref, 0)

    @pl.when(i < n_used)
    def _():
        slot = lax.rem(i, 2)
        start_gather(tok_next_ref, 1 - slot)
        wait_gather(slot)
        xb = _load_row_slabs(xbuf_ref.at[slot], EXPERT_ROWS, n_slab, BF16)
        de = o_ref.shape[1]
        for c in range(de // nc):
            sl = slice(c * nc, (c + 1) * nc)
            gate = jnp.dot(xb, wg_ref[0, :, sl], preferred_element_type=F32) + bg_ref[0, :, sl]
            lin = jnp.dot(xb, wl_ref[0, :, sl], preferred_element_type=F32) + bl_ref[0, :, sl]
            glu = jnp.minimum(gate, SWIGLU_LIMIT)
            lin = jnp.clip(lin, -SWIGLU_LIMIT, SWIGLU_LIMIT)
            act = glu * jax.nn.sigmoid(SWIGLU_ALPHA * glu) * (lin + 1.0)
            o_ref[:, sl] = act.astype(o_ref.dtype)

        @pl.when(i + 1 >= n_used)
        def _():
            wait_gather(1 - slot)

    @pl.when(i >= n_used)
    def _():
        o_ref[...] = jnp.zeros(o_ref.shape, o_ref.dtype)


def _expert_up(h, tok_blocks, blk_e, n_used, wg, wl, bg, bl, nc=512):
    d = wg.shape[1]
    de = wg.shape[2]
    n_blocks = tok_blocks.shape[0]

    def w_idx(i, be, nu):
        return (be[i], 0, 0)

    grid_spec = pltpu.PrefetchScalarGridSpec(
        num_scalar_prefetch=2,
        grid=(n_blocks,),
        in_specs=[
            pl.BlockSpec((1, 1, EXPERT_ROWS), lambda i, be, nu: (i, 0, 0), memory_space=pltpu.SMEM),
            pl.BlockSpec((1, 1, EXPERT_ROWS), lambda i, be, nu: (jnp.minimum(i + 1, n_blocks - 1), 0, 0),
                         memory_space=pltpu.SMEM),
            pl.BlockSpec(memory_space=pl.ANY),
            pl.BlockSpec((1, d, de), w_idx),
            pl.BlockSpec((1, d, de), w_idx),
            pl.BlockSpec((1, 1, de), w_idx),
            pl.BlockSpec((1, 1, de), w_idx),
        ],
        out_specs=pl.BlockSpec((EXPERT_ROWS, de), lambda i, be, nu: (i, 0)),
        scratch_shapes=[pltpu.VMEM((2, EXPERT_ROWS * (d // LANES), LANES), h.dtype),
                        pltpu.SemaphoreType.DMA((2,))],
    )
    return pl.pallas_call(
        functools.partial(_expert_up_kernel, nc=nc),
        grid_spec=grid_spec,
        out_shape=jax.ShapeDtypeStruct((n_blocks * EXPERT_ROWS, de), BF16),
        compiler_params=_cparams(56),
        name="moe_expert_up",
    )(blk_e, n_used, tok_blocks, tok_blocks, h, wg, wl, bg, bl)


def _expert_down_kernel(blk_e_ref, n_used_ref, a_ref, wd_ref, bd_ref, o_ref, wb_ref):
    i = pl.program_id(0)
    new_expert = jnp.logical_or(i == 0, blk_e_ref[i] != blk_e_ref[jnp.maximum(i - 1, 0)])

    @pl.when(jnp.logical_and(i < n_used_ref[0], new_expert))
    def _():
        wb_ref[...] = wd_ref[0].astype(BF16)

    @pl.when(i < n_used_ref[0])
    def _():
        _store_row_slabs(o_ref, jnp.dot(a_ref[...], wb_ref[...], preferred_element_type=F32) + bd_ref[0])

    @pl.when(i >= n_used_ref[0])
    def _():
        o_ref[...] = jnp.zeros(o_ref.shape, o_ref.dtype)


def _expert_down(act, blk_e, n_used, wd, bd):
    n_pad, de = act.shape
    d = wd.shape[2]
    n_blocks = n_pad // EXPERT_ROWS

    def row_idx(i, be, nu):
        return (jnp.minimum(i, nu[0] - 1), 0)

    def w_idx(i, be, nu):
        return (be[i], 0, 0)

    grid_spec = pltpu.PrefetchScalarGridSpec(
        num_scalar_prefetch=2,
        grid=(n_blocks,),
        in_specs=[
            pl.BlockSpec((EXPERT_ROWS, de), row_idx),
            pl.BlockSpec((1, de, d), w_idx),
            pl.BlockSpec((1, 1, d), w_idx),
        ],
        out_specs=pl.BlockSpec((EXPERT_ROWS * (d // LANES), LANES), lambda i, be, nu: (i, 0)),
        scratch_shapes=[pltpu.VMEM((de, d), BF16)],
    )
    return pl.pallas_call(
        _expert_down_kernel,
        grid_spec=grid_spec,
        out_shape=jax.ShapeDtypeStruct((n_pad * (d // LANES), LANES), F32),
        compiler_params=_cparams(56),
        name="moe_expert_down",
    )(blk_e, n_used, act, wd, bd)


def _combine_kernel(dest_ref, dest_next_ref, wt_ref, x_ref, gate_ref, ys_ref, o_ref, buf_ref, sem):
    i = pl.program_id(0)
    tm, d = x_ref.shape
    n_slab = d // LANES

    def row_copy(src_row, k, r, slot):
        src = pl.multiple_of(src_row * n_slab, n_slab)
        dst = pl.multiple_of(r * n_slab, n_slab)
        return pltpu.make_async_copy(ys_ref.at[pl.ds(src, n_slab)], buf_ref.at[slot, k, pl.ds(dst, n_slab)],
                                     sem.at[slot])

    def start_gather(idx_ref, slot):
        def issue(r, carry):
            for k in range(TOP_K):
                row_copy(idx_ref[0, 0, k * tm + r], k, r, slot).start()
            return carry

        lax.fori_loop(0, tm, issue, 0, unroll=8)

    def wait_gather(slot):
        def drain(r, carry):
            for k in range(TOP_K):
                row_copy(0, k, r, slot).wait()
            return carry

        lax.fori_loop(0, tm, drain, 0, unroll=8)

    slot = lax.rem(i, 2)

    @pl.when(i == 0)
    def _():
        start_gather(dest_ref, 0)

    @pl.when(i + 1 < pl.num_programs(0))
    def _():
        start_gather(dest_next_ref, 1 - slot)

    wait_gather(slot)
    for s in range(n_slab):
        cols = slice(s * LANES, (s + 1) * LANES)
        acc = wt_ref[:, 0:1] * buf_ref[slot, 0, pl.ds(s, tm, stride=n_slab), :]
        for k in range(1, TOP_K):
            acc = acc + wt_ref[:, k:k + 1] * buf_ref[slot, k, pl.ds(s, tm, stride=n_slab), :]
        o_ref[:, cols] = x_ref[:, cols] + gate_ref[0, :, cols] * acc


def _combine(ys, dest_blocks, wt, x_all, t_rows, gate, ridx, tm):
    d = x_all.shape[1]
    n_steps = t_rows // tm
    return pl.pallas_call(
        _combine_kernel,
        grid=(n_steps,),
        in_specs=[
            pl.BlockSpec((1, 1, TOP_K * tm), lambda i: (i, 0, 0), memory_space=pltpu.SMEM),
            pl.BlockSpec((1, 1, TOP_K * tm), lambda i: (jnp.minimum(i + 1, n_steps - 1), 0, 0),
                         memory_space=pltpu.SMEM),
            pl.BlockSpec((tm, TOP_K), lambda i: (i, 0)),
            pl.BlockSpec((tm, d), lambda i: (i, 0)),
            pl.BlockSpec((1, 1, d), lambda i: (ridx(i, tm), 0, 0)),
            pl.BlockSpec(memory_space=pl.ANY),
        ],
        out_specs=pl.BlockSpec((tm, d), lambda i: (i, 0)),
        out_shape=jax.ShapeDtypeStruct((t_rows, d), F32),
        scratch_shapes=[pltpu.VMEM((2, TOP_K, tm * (d // LANES), LANES), F32), pltpu.SemaphoreType.DMA((2,))],
        compiler_params=_cparams(32),
        name="moe_combine",
    )(dest_blocks, dest_blocks, wt, x_all, gate, ys)


def _slot_blocks(dest, tm):
    k, t = dest.shape
    return dest.reshape(k, t // tm, tm).transpose(1, 0, 2).reshape(t // tm, 1, k * tm)


def _moe(x_all, t_rows, norm_g, shift, scale, gate, ridx, w_r, b_r, layer, wg, wl, bg, bl, wd, bd, combine_tm=128):
    n_e = w_r.shape[1]
    h, top_idx, top_w = _router(x_all, t_rows, norm_g, shift, scale, w_r.T, b_r.reshape(n_e, 1), ridx)
    top_idx = top_idx[:TOP_K]
    top_w = top_w[:TOP_K]

    n_slots = TOP_K * t_rows
    e_flat = top_idx.reshape(n_slots)
    onehot = (e_flat[:, None] == jnp.arange(n_e, dtype=jnp.int32)[None, :]).astype(jnp.int32)
    csum = jnp.cumsum(onehot, axis=0)
    rank = jnp.sum(onehot * csum, axis=1) - 1
    counts = csum[-1]
    padded = (counts + EXPERT_ROWS - 1) // EXPERT_ROWS * EXPERT_ROWS
    pad_ends = jnp.cumsum(padded)
    pad_starts = pad_ends - padded
    dest = (jnp.sum(onehot * pad_starts[None, :], axis=1) + rank).reshape(TOP_K, t_rows)
    n_blocks = -(-(n_slots + n_e * (EXPERT_ROWS - 1)) // EXPERT_ROWS)
    n_pad = n_blocks * EXPERT_ROWS
    blk_row0 = jnp.arange(n_blocks, dtype=jnp.int32) * EXPERT_ROWS
    blk_e = jnp.minimum(jnp.sum((pad_ends[None, :] <= blk_row0[:, None]).astype(jnp.int32), axis=1), n_e - 1)
    n_used = (pad_ends[-1:] // EXPERT_ROWS).astype(jnp.int32)
    tok_of_slot = jnp.tile(jnp.arange(t_rows, dtype=jnp.int32), TOP_K)
    tok_blocks = jnp.zeros((n_pad,), jnp.int32).at[dest.reshape(n_slots)].set(
        tok_of_slot, unique_indices=True).reshape(n_blocks, 1, EXPERT_ROWS)
    blk_w = (blk_e + layer * n_e).astype(jnp.int32)

    act = _expert_up(h, tok_blocks, blk_w, n_used, wg, wl, bg, bl)
    ys = _expert_down(act, blk_w, n_used, wd, bd)
    return _combine(ys, _slot_blocks(dest, combine_tm), top_w.T, x_all, t_rows, gate, ridx, combine_tm)


def _rope_tables(seq, pad_rows):
    rows = seq // GRID_W
    row_pos = jnp.repeat(jnp.arange(rows), GRID_W).astype(F32)
    col_pos = jnp.tile(jnp.arange(GRID_W), rows).astype(F32)
    n_freq = HEAD_DIM // 4
    inv_freq = ROPE_THETA ** (-jnp.arange(n_freq, dtype=F32) / n_freq)
    ang_r = row_pos[:, None] * inv_freq
    ang_c = col_pos[:, None] * inv_freq
    cos = jnp.concatenate([jnp.cos(ang_r)] * 2 + [jnp.cos(ang_c)] * 2, axis=1)
    sin = jnp.concatenate([-jnp.sin(ang_r), jnp.sin(ang_r), -jnp.sin(ang_c), jnp.sin(ang_c)], axis=1)
    cos = jnp.concatenate([cos, jnp.ones((pad_rows, HEAD_DIM), F32)], axis=0)
    sin = jnp.concatenate([sin, jnp.zeros((pad_rows, HEAD_DIM), F32)], axis=0)
    return cos, sin


def kernel(x, c, ctx, c_ctx, w_mod, b_mod, norm_mix_g, norm_ffn_g, gmlp_w_in, gmlp_norm_g, gmlp_w_s, gmlp_b_s,
           gmlp_w_out, diff_w_qkv, diff_q_norm_g, diff_k_norm_g, diff_lambda, diff_subln_g, diff_w_o,
           moe_w_router, moe_b_router, moe_w_gate_up, moe_b_gate_up, moe_w_down, moe_b_down):
    batch, seq, d = x.shape
    ctx_len = ctx.shape[1]
    depth = w_mod.shape[0]
    t_lat = batch * seq
    t_ctx = batch * ctx_len
    t_all = t_lat + t_ctx
    assert batch < MOD_ROWS and depth == 2

    def ridx(i, tm):
        return jnp.minimum((i * tm) // seq, batch)

    c16 = jnp.concatenate([c, c_ctx[None, :], jnp.zeros((MOD_ROWS - batch - 1, d), F32)], axis=0)
    mods = _modulation(c16, w_mod, b_mod).reshape(depth, MOD_ROWS, N_MOD, d)

    def mod(layer, k):
        return mods[layer, :, k, :].reshape(MOD_ROWS, 1, d)

    x_all = jnp.concatenate([x.reshape(t_lat, d), ctx.reshape(t_ctx, d)], axis=0)

    n_e, de2 = moe_w_gate_up.shape[1], moe_w_gate_up.shape[3]
    de = de2 // 2
    wg, wl = _split_gate_up(moe_w_gate_up.reshape(depth * n_e, d, de2))
    bg = moe_b_gate_up[:, :, 0::2].reshape(depth * n_e, 1, de)
    bl = moe_b_gate_up[:, :, 1::2].reshape(depth * n_e, 1, de)
    wd = moe_w_down.reshape(depth * n_e, de, d)
    bd = moe_b_down.reshape(depth * n_e, 1, d)

    uv = _gmlp_in(x_all, norm_mix_g[0:1], mod(0, 0), mod(0, 1), gmlp_w_in[0].astype(BF16), ridx)
    x_all = _gmlp_out(uv, gmlp_norm_g[0:1], gmlp_w_s[0].astype(BF16), gmlp_b_s[0].T,
                      gmlp_w_out[0].astype(BF16), x_all, mod(0, 2), ridx)
    x_all = _moe(x_all, t_all, norm_ffn_g[0:1], mod(0, 3), mod(0, 4), mod(0, 5), ridx,
                 moe_w_router[0], moe_b_router[0], 0, wg, wl, bg, bl, wd, bd)

    lam_init = 0.8 - 0.6 * math.exp(-0.3 * 1)
    lf = diff_lambda[0].astype(F32)
    lam = (jnp.exp(jnp.sum(lf[0] * lf[1])) - jnp.exp(jnp.sum(lf[2] * lf[3])) + lam_init).reshape(1)
    tm_qkv = 1024
    cos_t, sin_t = _rope_tables(seq, tm_qkv)
    n_heads2 = d // HEAD_DIM
    q_scale = HEAD_DIM ** -0.5 * math.log2(math.e)
    head_gain = jnp.concatenate([jnp.tile(diff_q_norm_g[0] * q_scale, n_heads2),
                                 jnp.tile(diff_k_norm_g[0], n_heads2),
                                 jnp.ones((d,), F32)]).reshape(1, 3 * d)
    qkv = _qkv(x_all, t_all, t_lat, norm_mix_g[1:2], mod(1, 0), mod(1, 1), diff_w_qkv[0].astype(BF16), head_gain,
               cos_t, sin_t, ridx, seq, tm=tm_qkv)
    attn = _diff_attention(qkv, lam, diff_subln_g[0:1], batch, seq, ctx_len, d, 1.0 - lam_init)
    x_lat = _proj_residual(attn, diff_w_o[0].astype(BF16), x_all, mod(1, 2), ridx)
    x_lat = _moe(x_lat, t_lat, norm_ffn_g[1:2], mod(1, 3), mod(1, 4), mod(1, 5), ridx,
                 moe_w_router[1], moe_b_router[1], 1, wg, wl, bg, bl, wd, bd)
    return x_lat.reshape(batch, seq, d)
```

```python
import functools
import math

import jax
import jax.numpy as jnp
from jax import lax
from jax.experimental import pallas as pl
from jax.experimental.pallas import tpu as pltpu

F32 = jnp.float32
BF16 = jnp.bfloat16

NORM_EPS = 1e-6
N_MOD = 6
GRID_W = 64
GMLP_CHUNK = 128
GMLP_GROUPS = 8
DIFF_HEADS = 8
HEAD_DIM = 128
ROPE_THETA = 10000.0
N_EXPERTS = 32
TOP_K = 4
SWIGLU_ALPHA = 1.702
SWIGLU_LIMIT = 7.0

MOD_ROWS = 16
EXPERT_ROWS = 256
MIB = 1024 * 1024


def _cparams(vmem_mib):
    return pltpu.CompilerParams(vmem_limit_bytes=vmem_mib * MIB)


def _rms_mod(x, g, shift, scale):
    ms = jnp.mean(x * x, axis=-1, keepdims=True)
    y = x * lax.rsqrt(ms + NORM_EPS) * g
    return y * (1.0 + scale) + shift


def _mod_kernel(c_ref, w_ref, b_ref, o_ref):
    c = c_ref[...]
    s = c * jax.nn.sigmoid(c)
    o_ref[0] = jnp.dot(s.astype(BF16), w_ref[0].astype(BF16), preferred_element_type=F32) + b_ref[0]


def _modulation(c16, w_mod, b_mod):
    depth, d, n = w_mod.shape
    tn = 1024
    return pl.pallas_call(
        _mod_kernel,
        grid=(depth, n // tn),
        in_specs=[
            pl.BlockSpec((MOD_ROWS, d), lambda l, j: (0, 0)),
            pl.BlockSpec((1, d, tn), lambda l, j: (l, 0, j)),
            pl.BlockSpec((1, 1, tn), lambda l, j: (l, 0, j)),
        ],
        out_specs=pl.BlockSpec((1, MOD_ROWS, tn), lambda l, j: (l, 0, j)),
        out_shape=jax.ShapeDtypeStruct((depth, MOD_ROWS, n), F32),
        compiler_params=_cparams(40),
        name="modulation",
    )(c16, w_mod, b_mod.reshape(depth, 1, n))


def _gmlp_in_kernel(x_ref, g_ref, sh_ref, sc_ref, w_ref, o_ref, h_ref):
    @pl.when(pl.program_id(1) == 0)
    def _():
        h_ref[...] = _rms_mod(x_ref[...], g_ref[...], sh_ref[0], sc_ref[0]).astype(BF16)

    acc = jnp.dot(h_ref[...], w_ref[...], preferred_element_type=F32)
    gelu = 0.5 * acc * (1.0 + lax.erf(acc * (2.0 ** -0.5)))
    o_ref[...] = gelu.astype(o_ref.dtype)


def _gmlp_in(x, g, shift, scale, w_bf16, ridx, tm=1024, tn=1024):
    t, d = x.shape
    n = w_bf16.shape[1]
    return pl.pallas_call(
        _gmlp_in_kernel,
        grid=(t // tm, n // tn),
        in_specs=[
            pl.BlockSpec((tm, d), lambda i, j: (i, 0)),
            pl.BlockSpec((1, d), lambda i, j: (0, 0)),
            pl.BlockSpec((1, 1, d), lambda i, j: (ridx(i, tm), 0, 0)),
            pl.BlockSpec((1, 1, d), lambda i, j: (ridx(i, tm), 0, 0)),
            pl.BlockSpec((d, tn), lambda i, j: (0, j)),
        ],
        out_specs=pl.BlockSpec((tm, tn), lambda i, j: (i, j)),
        out_shape=jax.ShapeDtypeStruct((t, n), BF16),
        scratch_shapes=[pltpu.VMEM((tm, d), BF16)],
        compiler_params=_cparams(48),
        name="gmlp_in",
    )(x, g, shift, scale, w_bf16)


def _gmlp_out_kernel(u_ref, v_ref, lng_ref, ws_ref, bs_ref, wo_ref, x_ref, gate_ref, o_ref, t_ref):
    tm, width = v_ref.shape
    gw = width // GMLP_GROUPS
    v = v_ref[...].astype(F32)
    mu = jnp.mean(v, axis=-1, keepdims=True)
    vc = v - mu
    var = jnp.mean(vc * vc, axis=-1, keepdims=True)
    vn = (vc * lax.rsqrt(var + NORM_EPS) * lng_ref[...]).astype(BF16)
    for c in range(tm // GMLP_CHUNK):
        r0 = c * GMLP_CHUNK
        for g in range(GMLP_GROUPS):
            c0 = g * gw
            mixed = jnp.dot(ws_ref[g], vn[r0:r0 + GMLP_CHUNK, c0:c0 + gw], preferred_element_type=F32)
            mixed = mixed + bs_ref[:, g:g + 1]
            u = u_ref[r0:r0 + GMLP_CHUNK, c0:c0 + gw].astype(F32)
            t_ref[r0:r0 + GMLP_CHUNK, c0:c0 + gw] = (u * mixed).astype(BF16)
    y = jnp.dot(t_ref[...], wo_ref[...], preferred_element_type=F32)
    o_ref[...] = x_ref[...] + gate_ref[0] * y


def _gmlp_out(uv, ln_g, w_s_bf16, b_s_t, w_out_bf16, x, gate, ridx, tm=512):
    t, d = x.shape
    width = uv.shape[1] // 2
    return pl.pallas_call(
        _gmlp_out_kernel,
        grid=(t // tm,),
        in_specs=[
            pl.BlockSpec((tm, width), lambda i: (i, 0)),
            pl.BlockSpec((tm, width), lambda i: (i, 1)),
            pl.BlockSpec((1, width), lambda i: (0, 0)),
            pl.BlockSpec(w_s_bf16.shape, lambda i: (0, 0, 0)),
            pl.BlockSpec(b_s_t.shape, lambda i: (0, 0)),
            pl.BlockSpec((width, d), lambda i: (0, 0)),
            pl.BlockSpec((tm, d), lambda i: (i, 0)),
            pl.BlockSpec((1, 1, d), lambda i: (ridx(i, tm), 0, 0)),
        ],
        out_specs=pl.BlockSpec((tm, d), lambda i: (i, 0)),
        out_shape=jax.ShapeDtypeStruct((t, d), F32),
        scratch_shapes=[pltpu.VMEM((tm, width), BF16)],
        compiler_params=_cparams(56),
        name="gmlp_out",
    )(uv, uv, ln_g, w_s_bf16, b_s_t, w_out_bf16, x, gate)


def _qkv_kernel(x_ref, g_ref, sh_ref, sc_ref, w_ref, hg_ref, cos_ref, sin_ref, o_ref, h_ref, *, n_qk_tiles):
    j = pl.program_id(1)

    @pl.when(j == 0)
    def _():
        h_ref[...] = _rms_mod(x_ref[...], g_ref[...], sh_ref[0], sc_ref[0]).astype(BF16)

    acc = jnp.dot(h_ref[...], w_ref[...], preferred_element_type=F32)
    tm, tn = acc.shape

    @pl.when(j < n_qk_tiles)
    def _():
        cos = cos_ref[...]
        sin = sin_ref[...]
        first_half = (lax.broadcasted_iota(jnp.int32, (tm, HEAD_DIM), 1) % (HEAD_DIM // 2)) < (HEAD_DIM // 4)
        for hh in range(tn // HEAD_DIM):
            a = acc[:, hh * HEAD_DIM:(hh + 1) * HEAD_DIM]
            ms = jnp.mean(a * a, axis=-1, keepdims=True)
            a = a * lax.rsqrt(ms + NORM_EPS) * hg_ref[:, hh * HEAD_DIM:(hh + 1) * HEAD_DIM]
            partner = jnp.where(first_half,
                                pltpu.roll(a, HEAD_DIM - HEAD_DIM // 4, 1),
                                pltpu.roll(a, HEAD_DIM // 4, 1))
            o_ref[:, hh * HEAD_DIM:(hh + 1) * HEAD_DIM] = (a * cos + partner * sin).astype(o_ref.dtype)

    @pl.when(j >= n_qk_tiles)
    def _():
        o_ref[...] = acc.astype(o_ref.dtype)


def _qkv(x_all, t_rows, t_lat, g, shift, scale, w_bf16, head_gain, cos_t, sin_t, ridx, seq, tm=1024, tn=1024):
    d = x_all.shape[1]
    n = w_bf16.shape[1]
    n_qk_tiles = (2 * n // 3) // tn
    seq_blocks = seq // tm
    n_lat_blocks = t_lat // tm

    def rope_idx(i, j):
        return (jnp.where(i < n_lat_blocks, i % seq_blocks, seq_blocks), 0)

    return pl.pallas_call(
        functools.partial(_qkv_kernel, n_qk_tiles=n_qk_tiles),
        grid=(t_rows // tm, n // tn),
        in_specs=[
            pl.BlockSpec((tm, d), lambda i, j: (i, 0)),
            pl.BlockSpec((1, d), lambda i, j: (0, 0)),
            pl.BlockSpec((1, 1, d), lambda i, j: (ridx(i, tm), 0, 0)),
            pl.BlockSpec((1, 1, d), lambda i, j: (ridx(i, tm), 0, 0)),
            pl.BlockSpec((d, tn), lambda i, j: (0, j)),
            pl.BlockSpec((1, tn), lambda i, j: (0, j)),
            pl.BlockSpec((tm, HEAD_DIM), rope_idx),
            pl.BlockSpec((tm, HEAD_DIM), rope_idx),
        ],
        out_specs=pl.BlockSpec((tm, tn), lambda i, j: (i, j)),
        out_shape=jax.ShapeDtypeStruct((t_rows, n), BF16),
        scratch_shapes=[pltpu.VMEM((tm, d), BF16)],
        compiler_params=_cparams(48),
        name="qkv_proj",
    )(x_all, g, shift, scale, w_bf16, head_gain, cos_t, sin_t)


def _attn_kernel(lam_ref, q1_ref, q2_ref, k1l_ref, k2l_ref, k1c_ref, k2c_ref, vl_ref, vc_ref, sg_ref, o_ref,
                 *, out_scale, key_chunk):
    lam = lam_ref[0]
    nt = (((1,), (1,)), ((), ()))

    def softmax_times_v(q_ref, kc_ref, kl_ref):
        q = q_ref[...]
        tq = q.shape[0]
        chunks = [(kc_ref, vc_ref, c0, min(key_chunk, kc_ref.shape[0] - c0))
                  for c0 in range(0, kc_ref.shape[0], key_chunk)]
        chunks += [(kl_ref, vl_ref, c0, min(key_chunk, kl_ref.shape[0] - c0))
                   for c0 in range(0, kl_ref.shape[0], key_chunk)]
        m = jnp.full((tq, 1), -jnp.inf, F32)
        z = jnp.zeros((tq, 1), F32)
        acc = jnp.zeros((tq, vl_ref.shape[1]), F32)
        for k_ref, v_ref, c0, n in chunks:
            s = lax.dot_general(q, k_ref[c0:c0 + n, :], nt, preferred_element_type=F32)
            m_new = jnp.maximum(m, jnp.max(s, axis=-1, keepdims=True))
            alpha = jnp.exp2(m - m_new)
            p = jnp.exp2(s - m_new)
            z = alpha * z + jnp.sum(p, axis=-1, keepdims=True)
            acc = alpha * acc + jnp.dot(p.astype(BF16), v_ref[c0:c0 + n, :], preferred_element_type=F32)
            m = m_new
        return acc / z

    o = softmax_times_v(q1_ref, k1c_ref, k1l_ref) - lam * softmax_times_v(q2_ref, k2c_ref, k2l_ref)
    ms = jnp.mean(o * o, axis=-1, keepdims=True)
    o = o * lax.rsqrt(ms + NORM_EPS) * sg_ref[...] * out_scale
    o_ref[...] = o.astype(o_ref.dtype)


def _diff_attention(qkv, lam, subln_g, batch, seq, ctx_len, d, out_scale, tq=512, key_chunk=512):
    t_lat = batch * seq
    vd = 2 * HEAD_DIM
    qb = seq // tq
    k_col0 = d // HEAD_DIM
    v_col0 = 2 * d // vd
    ctx_row0 = t_lat // ctx_len

    return pl.pallas_call(
        functools.partial(_attn_kernel, out_scale=out_scale, key_chunk=key_chunk),
        grid=(batch, DIFF_HEADS, qb),
        in_specs=[
            pl.BlockSpec(memory_space=pltpu.SMEM),
            pl.BlockSpec((tq, HEAD_DIM), lambda b, h, q: (b * qb + q, 2 * h)),
            pl.BlockSpec((tq, HEAD_DIM), lambda b, h, q: (b * qb + q, 2 * h + 1)),
            pl.BlockSpec((seq, HEAD_DIM), lambda b, h, q: (b, k_col0 + 2 * h)),
            pl.BlockSpec((seq, HEAD_DIM), lambda b, h, q: (b, k_col0 + 2 * h + 1)),
            pl.BlockSpec((ctx_len, HEAD_DIM), lambda b, h, q: (ctx_row0 + b, k_col0 + 2 * h)),
            pl.BlockSpec((ctx_len, HEAD_DIM), lambda b, h, q: (ctx_row0 + b, k_col0 + 2 * h + 1)),
            pl.BlockSpec((seq, vd), lambda b, h, q: (b, v_col0 + h)),
            pl.BlockSpec((ctx_len, vd), lambda b, h, q: (ctx_row0 + b, v_col0 + h)),
            pl.BlockSpec((1, vd), lambda b, h, q: (0, 0)),
        ],
        out_specs=pl.BlockSpec((tq, vd), lambda b, h, q: (b * qb + q, h)),
        out_shape=jax.ShapeDtypeStruct((t_lat, d), BF16),
        compiler_params=_cparams(56),
        name="diff_attention",
    )(lam, qkv, qkv, qkv, qkv, qkv, qkv, qkv, qkv, subln_g)


def _proj_residual_kernel(a_ref, w_ref, x_ref, gate_ref, o_ref):
    y = jnp.dot(a_ref[...], w_ref[...], preferred_element_type=F32)
    o_ref[...] = x_ref[...] + gate_ref[0] * y


def _proj_residual(a, w_bf16, x_all, gate, ridx, tm=1024, tn=1024):
    t, k = a.shape
    n = w_bf16.shape[1]
    return pl.pallas_call(
        _proj_residual_kernel,
        grid=(t // tm, n // tn),
        in_specs=[
            pl.BlockSpec((tm, k), lambda i, j: (i, 0)),
            pl.BlockSpec((k, tn), lambda i, j: (0, j)),
            pl.BlockSpec((tm, tn), lambda i, j: (i, j)),
            pl.BlockSpec((1, 1, tn), lambda i, j: (ridx(i, tm), 0, j)),
        ],
        out_specs=pl.BlockSpec((tm, tn), lambda i, j: (i, j)),
        out_shape=jax.ShapeDtypeStruct((t, n), F32),
        compiler_params=_cparams(48),
        name="attn_out_proj",
    )(a, w_bf16, x_all, gate)


def _router_kernel(x_ref, g_ref, sh_ref, sc_ref, wr_ref, br_ref, h_ref, idx_ref, wt_ref):
    h = _rms_mod(x_ref[...], g_ref[...], sh_ref[0], sc_ref[0])
    h_ref[...] = h
    logits = lax.dot_general(wr_ref[...], h, (((1,), (1,)), ((), ())),
                             precision=lax.Precision.HIGHEST, preferred_element_type=F32) + br_ref[...]
    n_e, tm = logits.shape
    e_iota = lax.broadcasted_iota(jnp.int32, (n_e, tm), 0).astype(F32)
    vals, idxs = [], []
    for _ in range(TOP_K):
        m = jnp.max(logits, axis=0, keepdims=True)
        idx = jnp.min(jnp.where(logits == m, e_iota, float(n_e)), axis=0, keepdims=True)
        vals.append(m)
        idxs.append(idx)
        logits = jnp.where(e_iota == idx, -jnp.inf, logits)
    exps = [jnp.exp(v - vals[0]) for v in vals]
    denom = exps[0]
    for e in exps[1:]:
        denom = denom + e
    idx_ref[...] = jnp.zeros(idx_ref.shape, jnp.int32)
    wt_ref[...] = jnp.zeros(wt_ref.shape, F32)
    for k in range(TOP_K):
        idx_ref[k:k + 1, :] = idxs[k].astype(jnp.int32)
        wt_ref[k:k + 1, :] = exps[k] / denom


def _router(x_all, t_rows, g, shift, scale, w_r_t, b_r, ridx, tm=512):
    d = x_all.shape[1]
    n_e = w_r_t.shape[0]
    return pl.pallas_call(
        _router_kernel,
        grid=(t_rows // tm,),
        in_specs=[
            pl.BlockSpec((tm, d), lambda i: (i, 0)),
            pl.BlockSpec((1, d), lambda i: (0, 0)),
            pl.BlockSpec((1, 1, d), lambda i: (ridx(i, tm), 0, 0)),
            pl.BlockSpec((1, 1, d), lambda i: (ridx(i, tm), 0, 0)),
            pl.BlockSpec((n_e, d), lambda i: (0, 0)),
            pl.BlockSpec((n_e, 1), lambda i: (0, 0)),
        ],
        out_specs=[
            pl.BlockSpec((tm, d), lambda i: (i, 0)),
            pl.BlockSpec((8, tm), lambda i: (0, i)),
            pl.BlockSpec((8, tm), lambda i: (0, i)),
        ],
        out_shape=[
            jax.ShapeDtypeStruct((t_rows, d), F32),
            jax.ShapeDtypeStruct((8, t_rows), jnp.int32),
            jax.ShapeDtypeStruct((8, t_rows), F32),
        ],
        compiler_params=_cparams(40),
        name="moe_router",
    )(x_all, g, shift, scale, w_r_t, b_r)


PERM_TILE = 256


def _split_gate_up_tile(w_ref, p_ref, og_ref, ol_ref):
    half = PERM_TILE // 2
    tn = w_ref.shape[2]
    for c in range(tn // PERM_TILE):
        w = w_ref[0, :, c * PERM_TILE:(c + 1) * PERM_TILE].astype(BF16)
        t = jnp.dot(w, p_ref[...], preferred_element_type=F32)
        og_ref[0, :, c * half:(c + 1) * half] = t[:, :half].astype(BF16)
        ol_ref[0, :, c * half:(c + 1) * half] = t[:, half:].astype(BF16)


def _split_perm():
    half = PERM_TILE // 2
    col = jnp.arange(PERM_TILE, dtype=jnp.int32)
    src = jnp.where(col < half, 2 * col, 2 * (col - half) + 1)
    return (jnp.arange(PERM_TILE, dtype=jnp.int32)[:, None] == src[None, :]).astype(BF16)


def _split_gate_up(w_gu, n_e, tn=1024):
    _, d, n2 = w_gu.shape
    out = jax.ShapeDtypeStruct((n_e, d, n2 // 2), BF16)
    return pl.pallas_call(
        _split_gate_up_tile,
        grid=(n_e, n2 // tn),
        in_specs=[
            pl.BlockSpec((1, d, tn), lambda e, j: (e, 0, j)),
            pl.BlockSpec((PERM_TILE, PERM_TILE), lambda e, j: (0, 0)),
        ],
        out_specs=[
            pl.BlockSpec((1, d, tn // 2), lambda e, j: (e, 0, j)),
            pl.BlockSpec((1, d, tn // 2), lambda e, j: (e, 0, j)),
        ],
        out_shape=[out, out],
        compiler_params=_cparams(40),
        name="moe_split_gate_up",
    )(w_gu, _split_perm())


SIDE_SPLIT_TILE = 512


def _expert_up_kernel(blk_e_ref, n_used_ref, tok_ref, tok_next_ref, h_ref, wg_ref, wl_ref, bg_ref, bl_ref,
                      *rest, nc, side_split):
    if side_split:
        ws_ref, p_ref, o_ref, sg_ref, sl_ref, xbuf_ref, sem = rest
    else:
        o_ref, xbuf_ref, sem = rest
    i = pl.program_id(0)
    n_used = n_used_ref[0]

    def row_copy(src_row, r, slot):
        return pltpu.make_async_copy(h_ref.at[pl.ds(src_row, 1)], xbuf_ref.at[slot, pl.ds(r, 1)], sem.at[slot])

    def start_gather(idx_ref, slot):
        for r in range(EXPERT_ROWS):
            row_copy(idx_ref[0, 0, r], r, slot).start()

    def wait_gather(slot):
        for r in range(EXPERT_ROWS):
            row_copy(0, r, slot).wait()

    @pl.when(i == 0)
    def _():
        start_gather(tok_ref, 0)

    @pl.when(i < n_used)
    def _():
        slot = lax.rem(i, 2)
        start_gather(tok_next_ref, 1 - slot)
        wait_gather(slot)
        xb = xbuf_ref[slot].astype(BF16)
        de = o_ref.shape[1]
        for c in range(de // nc):
            sl = slice(c * nc, (c + 1) * nc)
            gate = jnp.dot(xb, wg_ref[0, :, sl], preferred_element_type=F32) + bg_ref[0, :, sl]
            lin = jnp.dot(xb, wl_ref[0, :, sl], preferred_element_type=F32) + bl_ref[0, :, sl]
            glu = jnp.minimum(gate, SWIGLU_LIMIT)
            lin = jnp.clip(lin, -SWIGLU_LIMIT, SWIGLU_LIMIT)
            act = glu * jax.nn.sigmoid(SWIGLU_ALPHA * glu) * (lin + 1.0)
            o_ref[:, sl] = act.astype(o_ref.dtype)

        if side_split:
            _split_gate_up_tile(ws_ref, p_ref, sg_ref, sl_ref)

        @pl.when(i + 1 >= n_used)
        def _():
            wait_gather(1 - slot)

    @pl.when(i >= n_used)
    def _():
        o_ref[...] = jnp.zeros(o_ref.shape, o_ref.dtype)


def _expert_up(h, tok_blocks, blk_e, n_used, wg, wl, bg, bl, n_slots, split_next=None, nc=512):
    d = wg.shape[1]
    de = wg.shape[2]
    n_blocks = tok_blocks.shape[0]

    def w_idx(i, be, nu):
        return (be[i], 0, 0)

    in_specs = [
        pl.BlockSpec((1, 1, EXPERT_ROWS), lambda i, be, nu: (i, 0, 0), memory_space=pltpu.SMEM),
        pl.BlockSpec((1, 1, EXPERT_ROWS), lambda i, be, nu: (jnp.minimum(i + 1, n_blocks - 1), 0, 0),
                     memory_space=pltpu.SMEM),
        pl.BlockSpec(memory_space=pl.ANY),
        pl.BlockSpec((1, d, de), w_idx),
        pl.BlockSpec((1, d, de), w_idx),
        pl.BlockSpec((1, 1, de), w_idx),
        pl.BlockSpec((1, 1, de), w_idx),
    ]
    out_specs = [pl.BlockSpec((EXPERT_ROWS, de), lambda i, be, nu: (i, 0))]
    out_shape = [jax.ShapeDtypeStruct((n_blocks * EXPERT_ROWS, de), BF16)]
    operands = [tok_blocks, tok_blocks, h, wg, wl, bg, bl]
    if split_next is not None:
        w_raw, e0, n_e = split_next
        tiles_per_e = w_raw.shape[2] // SIDE_SPLIT_TILE
        n_tiles = n_e * tiles_per_e
        assert n_slots // EXPERT_ROWS >= n_tiles

        def tile_idx(i, be, nu, first):
            t = jnp.minimum(i, n_tiles - 1)
            return (first + t // tiles_per_e, 0, t % tiles_per_e)

        in_specs += [
            pl.BlockSpec((1, d, SIDE_SPLIT_TILE), functools.partial(tile_idx, first=e0)),
            pl.BlockSpec((PERM_TILE, PERM_TILE), lambda i, be, nu: (0, 0)),
        ]
        split_out = jax.ShapeDtypeStruct((n_e, d, w_raw.shape[2] // 2), BF16)
        out_specs += [pl.BlockSpec((1, d, SIDE_SPLIT_TILE // 2), functools.partial(tile_idx, first=0))] * 2
        out_shape += [split_out, split_out]
        operands += [w_raw, _split_perm()]

    grid_spec = pltpu.PrefetchScalarGridSpec(
        num_scalar_prefetch=2,
        grid=(n_blocks,),
        in_specs=in_specs,
        out_specs=out_specs,
        scratch_shapes=[pltpu.VMEM((2, EXPERT_ROWS, d), h.dtype), pltpu.SemaphoreType.DMA((2,))],
    )
    return pl.pallas_call(
        functools.partial(_expert_up_kernel, nc=nc, side_split=split_next is not None),
        grid_spec=grid_spec,
        out_shape=out_shape,
        compiler_params=_cparams(56),
        name="moe_expert_up",
    )(blk_e, n_used, *operands)


def _expert_down_kernel(blk_e_ref, n_used_ref, a_ref, wd_ref, bd_ref, o_ref, wb_ref):
    i = pl.program_id(0)
    new_expert = jnp.logical_or(i == 0, blk_e_ref[i] != blk_e_ref[jnp.maximum(i - 1, 0)])

    @pl.when(jnp.logical_and(i < n_used_ref[0], new_expert))
    def _():
        wb_ref[...] = wd_ref[0].astype(BF16)

    @pl.when(i < n_used_ref[0])
    def _():
        o_ref[...] = jnp.dot(a_ref[...], wb_ref[...], preferred_element_type=F32) + bd_ref[0]

    @pl.when(i >= n_used_ref[0])
    def _():
        o_ref[...] = jnp.zeros(o_ref.shape, o_ref.dtype)


def _expert_down(act, blk_e, n_used, wd, bd):
    n_pad, de = act.shape
    d = wd.shape[2]
    n_blocks = n_pad // EXPERT_ROWS

    def row_idx(i, be, nu):
        return (jnp.minimum(i, nu[0] - 1), 0)

    def w_idx(i, be, nu):
        return (be[i], 0, 0)

    grid_spec = pltpu.PrefetchScalarGridSpec(
        num_scalar_prefetch=2,
        grid=(n_blocks,),
        in_specs=[
            pl.BlockSpec((EXPERT_ROWS, de), row_idx),
            pl.BlockSpec((1, de, d), w_idx),
            pl.BlockSpec((1, 1, d), w_idx),
        ],
        out_specs=pl.BlockSpec((EXPERT_ROWS, d), lambda i, be, nu: (i, 0)),
        scratch_shapes=[pltpu.VMEM((de, d), BF16)],
    )
    return pl.pallas_call(
        _expert_down_kernel,
        grid_spec=grid_spec,
        out_shape=jax.ShapeDtypeStruct((n_pad, d), F32),
        compiler_params=_cparams(56),
        name="moe_expert_down",
    )(blk_e, n_used, act, wd, bd)


def _combine_kernel(dest_ref, dest_next_ref, wt_ref, x_ref, gate_ref, ys_ref, o_ref, buf_ref, sem):
    i = pl.program_id(0)
    tm = x_ref.shape[0]

    def row_copy(src_row, k, r, slot):
        return pltpu.make_async_copy(ys_ref.at[pl.ds(src_row, 1)], buf_ref.at[slot, k, pl.ds(r, 1)], sem.at[slot])

    def start_gather(idx_ref, slot):
        def issue(r, carry):
            for k in range(TOP_K):
                row_copy(idx_ref[0, 0, k * tm + r], k, r, slot).start()
            return carry

        lax.fori_loop(0, tm, issue, 0, unroll=8)

    def wait_gather(slot):
        def drain(r, carry):
            for k in range(TOP_K):
                row_copy(0, k, r, slot).wait()
            return carry

        lax.fori_loop(0, tm, drain, 0, unroll=8)

    slot = lax.rem(i, 2)

    @pl.when(i == 0)
    def _():
        start_gather(dest_ref, 0)

    @pl.when(i + 1 < pl.num_programs(0))
    def _():
        start_gather(dest_next_ref, 1 - slot)

    wait_gather(slot)
    acc = wt_ref[:, 0:1] * buf_ref[slot, 0]
    for k in range(1, TOP_K):
        acc = acc + wt_ref[:, k:k + 1] * buf_ref[slot, k]
    o_ref[...] = x_ref[...] + gate_ref[0] * acc


def _combine(ys, dest_blocks, wt, x_all, t_rows, gate, ridx, tm):
    d = x_all.shape[1]
    n_steps = t_rows // tm
    return pl.pallas_call(
        _combine_kernel,
        grid=(n_steps,),
        in_specs=[
            pl.BlockSpec((1, 1, TOP_K * tm), lambda i: (i, 0, 0), memory_space=pltpu.SMEM),
            pl.BlockSpec((1, 1, TOP_K * tm), lambda i: (jnp.minimum(i + 1, n_steps - 1), 0, 0),
                         memory_space=pltpu.SMEM),
            pl.BlockSpec((tm, TOP_K), lambda i: (i, 0)),
            pl.BlockSpec((tm, d), lambda i: (i, 0)),
            pl.BlockSpec((1, 1, d), lambda i: (ridx(i, tm), 0, 0)),
            pl.BlockSpec(memory_space=pl.ANY),
        ],
        out_specs=pl.BlockSpec((tm, d), lambda i: (i, 0)),
        out_shape=jax.ShapeDtypeStruct((t_rows, d), F32),
        scratch_shapes=[pltpu.VMEM((2, TOP_K, tm, d), F32), pltpu.SemaphoreType.DMA((2,))],
        compiler_params=_cparams(32),
        name="moe_combine",
    )(dest_blocks, dest_blocks, wt, x_all, gate, ys)


def _slot_blocks(dest, tm):
    k, t = dest.shape
    return dest.reshape(k, t // tm, tm).transpose(1, 0, 2).reshape(t // tm, 1, k * tm)


def _moe(x_all, t_rows, norm_g, shift, scale, gate, ridx, w_r, b_r, layer, wg, wl, bg, bl, wd, bd,
         split_next=None, combine_tm=128):
    n_e = w_r.shape[1]
    h, top_idx, top_w = _router(x_all, t_rows, norm_g, shift, scale, w_r.T, b_r.reshape(n_e, 1), ridx)
    top_idx = top_idx[:TOP_K]
    top_w = top_w[:TOP_K]

    n_slots = TOP_K * t_rows
    e_flat = top_idx.reshape(n_slots)
    onehot = (e_flat[:, None] == jnp.arange(n_e, dtype=jnp.int32)[None, :]).astype(jnp.int32)
    csum = jnp.cumsum(onehot, axis=0)
    rank = jnp.sum(onehot * csum, axis=1) - 1
    counts = csum[-1]
    padded = (counts + EXPERT_ROWS - 1) // EXPERT_ROWS * EXPERT_ROWS
    pad_ends = jnp.cumsum(padded)
    pad_starts = pad_ends - padded
    dest = (jnp.sum(onehot * pad_starts[None, :], axis=1) + rank).reshape(TOP_K, t_rows)
    n_blocks = -(-(n_slots + n_e * (EXPERT_ROWS - 1)) // EXPERT_ROWS)
    n_pad = n_blocks * EXPERT_ROWS
    blk_row0 = jnp.arange(n_blocks, dtype=jnp.int32) * EXPERT_ROWS
    blk_e = jnp.minimum(jnp.sum((pad_ends[None, :] <= blk_row0[:, None]).astype(jnp.int32), axis=1), n_e - 1)
    n_used = (pad_ends[-1:] // EXPERT_ROWS).astype(jnp.int32)
    tok_of_slot = jnp.tile(jnp.arange(t_rows, dtype=jnp.int32), TOP_K)
    tok_blocks = jnp.zeros((n_pad,), jnp.int32).at[dest.reshape(n_slots)].set(
        tok_of_slot, unique_indices=True).reshape(n_blocks, 1, EXPERT_ROWS)
    blk_e = blk_e.astype(jnp.int32)

    up = _expert_up(h, tok_blocks, blk_e, n_used, wg, wl, bg, bl, n_slots, split_next)
    ys = _expert_down(up[0], blk_e + layer * n_e, n_used, wd, bd)
    out = _combine(ys, _slot_blocks(dest, combine_tm), top_w.T, x_all, t_rows, gate, ridx, combine_tm)
    return out, tuple(up[1:])


def _rope_tables(seq, pad_rows):
    rows = seq // GRID_W
    row_pos = jnp.repeat(jnp.arange(rows), GRID_W).astype(F32)
    col_pos = jnp.tile(jnp.arange(GRID_W), rows).astype(F32)
    n_freq = HEAD_DIM // 4
    inv_freq = ROPE_THETA ** (-jnp.arange(n_freq, dtype=F32) / n_freq)
    ang_r = row_pos[:, None] * inv_freq
    ang_c = col_pos[:, None] * inv_freq
    cos = jnp.concatenate([jnp.cos(ang_r)] * 2 + [jnp.cos(ang_c)] * 2, axis=1)
    sin = jnp.concatenate([-jnp.sin(ang_r), jnp.sin(ang_r), -jnp.sin(ang_c), jnp.sin(ang_c)], axis=1)
    cos = jnp.concatenate([cos, jnp.ones((pad_rows, HEAD_DIM), F32)], axis=0)
    sin = jnp.concatenate([sin, jnp.zeros((pad_rows, HEAD_DIM), F32)], axis=0)
    return cos, sin


def kernel(x, c, ctx, c_ctx, w_mod, b_mod, norm_mix_g, norm_ffn_g, gmlp_w_in, gmlp_norm_g, gmlp_w_s, gmlp_b_s,
           gmlp_w_out, diff_w_qkv, diff_q_norm_g, diff_k_norm_g, diff_lambda, diff_subln_g, diff_w_o,
           moe_w_router, moe_b_router, moe_w_gate_up, moe_b_gate_up, moe_w_down, moe_b_down):
    batch, seq, d = x.shape
    ctx_len = ctx.shape[1]
    depth = w_mod.shape[0]
    t_lat = batch * seq
    t_ctx = batch * ctx_len
    t_all = t_lat + t_ctx
    assert batch < MOD_ROWS and depth == 2

    def ridx(i, tm):
        return jnp.minimum((i * tm) // seq, batch)

    c16 = jnp.concatenate([c, c_ctx[None, :], jnp.zeros((MOD_ROWS - batch - 1, d), F32)], axis=0)
    mods = _modulation(c16, w_mod, b_mod).reshape(depth, MOD_ROWS, N_MOD, d)

    def mod(layer, k):
        return mods[layer, :, k, :].reshape(MOD_ROWS, 1, d)

    x_all = jnp.concatenate([x.reshape(t_lat, d), ctx.reshape(t_ctx, d)], axis=0)

    n_e, de2 = moe_w_gate_up.shape[1], moe_w_gate_up.shape[3]
    de = de2 // 2
    w_gu_all = moe_w_gate_up.reshape(depth * n_e, d, de2)
    wg0, wl0 = _split_gate_up(w_gu_all, n_e)
    bg = moe_b_gate_up[:, :, 0::2].reshape(depth, n_e, 1, de)
    bl = moe_b_gate_up[:, :, 1::2].reshape(depth, n_e, 1, de)
    wd = moe_w_down.reshape(depth * n_e, de, d)
    bd = moe_b_down.reshape(depth * n_e, 1, d)

    uv = _gmlp_in(x_all, norm_mix_g[0:1], mod(0, 0), mod(0, 1), gmlp_w_in[0].astype(BF16), ridx)
    x_all = _gmlp_out(uv, gmlp_norm_g[0:1], gmlp_w_s[0].astype(BF16), gmlp_b_s[0].T,
                      gmlp_w_out[0].astype(BF16), x_all, mod(0, 2), ridx)
    x_all, (wg1, wl1) = _moe(x_all, t_all, norm_ffn_g[0:1], mod(0, 3), mod(0, 4), mod(0, 5), ridx,
                             moe_w_router[0], moe_b_router[0], 0, wg0, wl0, bg[0], bl[0], wd, bd,
                             split_next=(w_gu_all, n_e, n_e))

    lam_init = 0.8 - 0.6 * math.exp(-0.3 * 1)
    lf = diff_lambda[0].astype(F32)
    lam = (jnp.exp(jnp.sum(lf[0] * lf[1])) - jnp.exp(jnp.sum(lf[2] * lf[3])) + lam_init).reshape(1)
    tm_qkv = 1024
    cos_t, sin_t = _rope_tables(seq, tm_qkv)
    n_heads2 = d // HEAD_DIM
    q_scale = HEAD_DIM ** -0.5 * math.log2(math.e)
    head_gain = jnp.concatenate([jnp.tile(diff_q_norm_g[0] * q_scale, n_heads2),
                                 jnp.tile(diff_k_norm_g[0], n_heads2),
                                 jnp.ones((d,), F32)]).reshape(1, 3 * d)
    qkv = _qkv(x_all, t_all, t_lat, norm_mix_g[1:2], mod(1, 0), mod(1, 1), diff_w_qkv[0].astype(BF16), head_gain,
               cos_t, sin_t, ridx, seq, tm=tm_qkv)
    attn = _diff_attention(qkv, lam, diff_subln_g[0:1], batch, seq, ctx_len, d, 1.0 - lam_init)
    x_lat = _proj_residual(attn, diff_w_o[0].astype(BF16), x_all, mod(1, 2), ridx)
    x_lat, _ = _moe(x_lat, t_lat, norm_ffn_g[1:2], mod(1, 3), mod(1, 4), mod(1, 5), ridx,
                    moe_w_router[1], moe_b_router[1], 1, wg1, wl1, bg[1], bl[1], wd, bd)
    return x_lat.reshape(batch, seq, d)
```

```python
import functools
import math

import jax
import jax.numpy as jnp
from jax import lax
from jax.experimental import pallas as pl
from jax.experimental.pallas import tpu as pltpu

F32 = jnp.float32
BF16 = jnp.bfloat16

NORM_EPS = 1e-6
N_MOD = 6
GRID_W = 64
GMLP_CHUNK = 128
GMLP_GROUPS = 8
DIFF_HEADS = 8
HEAD_DIM = 128
ROPE_THETA = 10000.0
N_EXPERTS = 32
TOP_K = 4
SWIGLU_ALPHA = 1.702
SWIGLU_LIMIT = 7.0

MOD_ROWS = 16
EXPERT_ROWS = 256
MIB = 1024 * 1024


def _cparams(vmem_mib):
    return pltpu.CompilerParams(vmem_limit_bytes=vmem_mib * MIB)


def _rms_mod(x, g, shift, scale):
    ms = jnp.mean(x * x, axis=-1, keepdims=True)
    y = x * lax.rsqrt(ms + NORM_EPS) * g
    return y * (1.0 + scale) + shift


def _mod_kernel(c_ref, w_ref, b_ref, o_ref):
    c = c_ref[...]
    s = c * jax.nn.sigmoid(c)
    o_ref[0] = jnp.dot(s.astype(BF16), w_ref[0].astype(BF16), preferred_element_type=F32) + b_ref[0]


def _modulation(c16, w_mod, b_mod):
    depth, d, n = w_mod.shape
    tn = 1024
    return pl.pallas_call(
        _mod_kernel,
        grid=(depth, n // tn),
        in_specs=[
            pl.BlockSpec((MOD_ROWS, d), lambda l, j: (0, 0)),
            pl.BlockSpec((1, d, tn), lambda l, j: (l, 0, j)),
            pl.BlockSpec((1, 1, tn), lambda l, j: (l, 0, j)),
        ],
        out_specs=pl.BlockSpec((1, MOD_ROWS, tn), lambda l, j: (l, 0, j)),
        out_shape=jax.ShapeDtypeStruct((depth, MOD_ROWS, n), F32),
        compiler_params=_cparams(40),
        name="modulation",
    )(c16, w_mod, b_mod.reshape(depth, 1, n))


def _gmlp_in_kernel(x_ref, g_ref, sh_ref, sc_ref, w_ref, o_ref, h_ref):
    @pl.when(pl.program_id(1) == 0)
    def _():
        h_ref[...] = _rms_mod(x_ref[...], g_ref[...], sh_ref[0], sc_ref[0]).astype(BF16)

    acc = jnp.dot(h_ref[...], w_ref[...], preferred_element_type=F32)
    gelu = 0.5 * acc * (1.0 + lax.erf(acc * (2.0 ** -0.5)))
    o_ref[...] = gelu.astype(o_ref.dtype)


def _gmlp_in(x, g, shift, scale, w_bf16, ridx, tm=1024, tn=1024):
    t, d = x.shape
    n = w_bf16.shape[1]
    return pl.pallas_call(
        _gmlp_in_kernel,
        grid=(t // tm, n // tn),
        in_specs=[
            pl.BlockSpec((tm, d), lambda i, j: (i, 0)),
            pl.BlockSpec((1, d), lambda i, j: (0, 0)),
            pl.BlockSpec((1, 1, d), lambda i, j: (ridx(i, tm), 0, 0)),
            pl.BlockSpec((1, 1, d), lambda i, j: (ridx(i, tm), 0, 0)),
            pl.BlockSpec((d, tn), lambda i, j: (0, j)),
        ],
        out_specs=pl.BlockSpec((tm, tn), lambda i, j: (i, j)),
        out_shape=jax.ShapeDtypeStruct((t, n), BF16),
        scratch_shapes=[pltpu.VMEM((tm, d), BF16)],
        compiler_params=_cparams(48),
        name="gmlp_in",
    )(x, g, shift, scale, w_bf16)


def _gmlp_out_kernel(u_ref, v_ref, lng_ref, ws_ref, bs_ref, wo_ref, x_ref, gate_ref, o_ref, t_ref):
    tm, width = v_ref.shape
    gw = width // GMLP_GROUPS
    v = v_ref[...].astype(F32)
    mu = jnp.mean(v, axis=-1, keepdims=True)
    vc = v - mu
    var = jnp.mean(vc * vc, axis=-1, keepdims=True)
    vn = (vc * lax.rsqrt(var + NORM_EPS) * lng_ref[...]).astype(BF16)
    for c in range(tm // GMLP_CHUNK):
        r0 = c * GMLP_CHUNK
        for g in range(GMLP_GROUPS):
            c0 = g * gw
            mixed = jnp.dot(ws_ref[g], vn[r0:r0 + GMLP_CHUNK, c0:c0 + gw], preferred_element_type=F32)
            mixed = mixed + bs_ref[:, g:g + 1]
            u = u_ref[r0:r0 + GMLP_CHUNK, c0:c0 + gw].astype(F32)
            t_ref[r0:r0 + GMLP_CHUNK, c0:c0 + gw] = (u * mixed).astype(BF16)
    y = jnp.dot(t_ref[...], wo_ref[...], preferred_element_type=F32)
    o_ref[...] = x_ref[...] + gate_ref[0] * y


def _gmlp_out(uv, ln_g, w_s_bf16, b_s_t, w_out_bf16, x, gate, ridx, tm=512):
    t, d = x.shape
    width = uv.shape[1] // 2
    return pl.pallas_call(
        _gmlp_out_kernel,
        grid=(t // tm,),
        in_specs=[
            pl.BlockSpec((tm, width), lambda i: (i, 0)),
            pl.BlockSpec((tm, width), lambda i: (i, 1)),
            pl.BlockSpec((1, width), lambda i: (0, 0)),
            pl.BlockSpec(w_s_bf16.shape, lambda i: (0, 0, 0)),
            pl.BlockSpec(b_s_t.shape, lambda i: (0, 0)),
            pl.BlockSpec((width, d), lambda i: (0, 0)),
            pl.BlockSpec((tm, d), lambda i: (i, 0)),
            pl.BlockSpec((1, 1, d), lambda i: (ridx(i, tm), 0, 0)),
        ],
        out_specs=pl.BlockSpec((tm, d), lambda i: (i, 0)),
        out_shape=jax.ShapeDtypeStruct((t, d), F32),
        scratch_shapes=[pltpu.VMEM((tm, width), BF16)],
        compiler_params=_cparams(56),
        name="gmlp_out",
    )(uv, uv, ln_g, w_s_bf16, b_s_t, w_out_bf16, x, gate)


def _qkv_kernel(x_ref, g_ref, sh_ref, sc_ref, w_ref, hg_ref, cos_ref, sin_ref, o_ref, h_ref, *, n_qk_tiles):
    j = pl.program_id(1)

    @pl.when(j == 0)
    def _():
        h_ref[...] = _rms_mod(x_ref[...], g_ref[...], sh_ref[0], sc_ref[0]).astype(BF16)

    acc = jnp.dot(h_ref[...], w_ref[...], preferred_element_type=F32)
    tm, tn = acc.shape

    @pl.when(j < n_qk_tiles)
    def _():
        cos = cos_ref[...]
        sin = sin_ref[...]
        first_half = (lax.broadcasted_iota(jnp.int32, (tm, HEAD_DIM), 1) % (HEAD_DIM // 2)) < (HEAD_DIM // 4)
        for hh in range(tn // HEAD_DIM):
            a = acc[:, hh * HEAD_DIM:(hh + 1) * HEAD_DIM]
            ms = jnp.mean(a * a, axis=-1, keepdims=True)
            a = a * lax.rsqrt(ms + NORM_EPS) * hg_ref[:, hh * HEAD_DIM:(hh + 1) * HEAD_DIM]
            partner = jnp.where(first_half,
                                pltpu.roll(a, HEAD_DIM - HEAD_DIM // 4, 1),
                                pltpu.roll(a, HEAD_DIM // 4, 1))
            o_ref[:, hh * HEAD_DIM:(hh + 1) * HEAD_DIM] = (a * cos + partner * sin).astype(o_ref.dtype)

    @pl.when(j >= n_qk_tiles)
    def _():
        o_ref[...] = acc.astype(o_ref.dtype)


def _qkv(x_all, t_rows, t_lat, g, shift, scale, w_bf16, head_gain, cos_t, sin_t, ridx, seq, tm=1024, tn=1024):
    d = x_all.shape[1]
    n = w_bf16.shape[1]
    n_qk_tiles = (2 * n // 3) // tn
    seq_blocks = seq // tm
    n_lat_blocks = t_lat // tm

    def rope_idx(i, j):
        return (jnp.where(i < n_lat_blocks, i % seq_blocks, seq_blocks), 0)

    return pl.pallas_call(
        functools.partial(_qkv_kernel, n_qk_tiles=n_qk_tiles),
        grid=(t_rows // tm, n // tn),
        in_specs=[
            pl.BlockSpec((tm, d), lambda i, j: (i, 0)),
            pl.BlockSpec((1, d), lambda i, j: (0, 0)),
            pl.BlockSpec((1, 1, d), lambda i, j: (ridx(i, tm), 0, 0)),
            pl.BlockSpec((1, 1, d), lambda i, j: (ridx(i, tm), 0, 0)),
            pl.BlockSpec((d, tn), lambda i, j: (0, j)),
            pl.BlockSpec((1, tn), lambda i, j: (0, j)),
            pl.BlockSpec((tm, HEAD_DIM), rope_idx),
            pl.BlockSpec((tm, HEAD_DIM), rope_idx),
        ],
        out_specs=pl.BlockSpec((tm, tn), lambda i, j: (i, j)),
        out_shape=jax.ShapeDtypeStruct((t_rows, n), BF16),
        scratch_shapes=[pltpu.VMEM((tm, d), BF16)],
        compiler_params=_cparams(48),
        name="qkv_proj",
    )(x_all, g, shift, scale, w_bf16, head_gain, cos_t, sin_t)


def _attn_kernel(lam_ref, q1_ref, q2_ref, k1l_ref, k2l_ref, k1c_ref, k2c_ref, vl_ref, vc_ref, sg_ref, o_ref,
                 *, out_scale, key_chunk):
    lam = lam_ref[0]
    nt = (((1,), (1,)), ((), ()))

    def softmax_times_v(q_ref, kc_ref, kl_ref):
        q = q_ref[...]
        tq = q.shape[0]
        chunks = [(kc_ref, vc_ref, c0, min(key_chunk, kc_ref.shape[0] - c0))
                  for c0 in range(0, kc_ref.shape[0], key_chunk)]
        chunks += [(kl_ref, vl_ref, c0, min(key_chunk, kl_ref.shape[0] - c0))
                   for c0 in range(0, kl_ref.shape[0], key_chunk)]
        m = jnp.full((tq, 1), -jnp.inf, F32)
        z = jnp.zeros((tq, 1), F32)
        acc = jnp.zeros((tq, vl_ref.shape[1]), F32)
        for k_ref, v_ref, c0, n in chunks:
            s = lax.dot_general(q, k_ref[c0:c0 + n, :], nt, preferred_element_type=F32)
            m_new = jnp.maximum(m, jnp.max(s, axis=-1, keepdims=True))
            alpha = jnp.exp2(m - m_new)
            p = jnp.exp2(s - m_new)
            z = alpha * z + jnp.sum(p, axis=-1, keepdims=True)
            acc = alpha * acc + jnp.dot(p.astype(BF16), v_ref[c0:c0 + n, :], preferred_element_type=F32)
            m = m_new
        return acc / z

    o = softmax_times_v(q1_ref, k1c_ref, k1l_ref) - lam * softmax_times_v(q2_ref, k2c_ref, k2l_ref)
    ms = jnp.mean(o * o, axis=-1, keepdims=True)
    o = o * lax.rsqrt(ms + NORM_EPS) * sg_ref[...] * out_scale
    o_ref[...] = o.astype(o_ref.dtype)


def _diff_attention(qkv, lam, subln_g, batch, seq, ctx_len, d, out_scale, tq=512, key_chunk=512):
    t_lat = batch * seq
    vd = 2 * HEAD_DIM
    qb = seq // tq
    k_col0 = d // HEAD_DIM
    v_col0 = 2 * d // vd
    ctx_row0 = t_lat // ctx_len

    return pl.pallas_call(
        functools.partial(_attn_kernel, out_scale=out_scale, key_chunk=key_chunk),
        grid=(batch, DIFF_HEADS, qb),
        in_specs=[
            pl.BlockSpec(memory_space=pltpu.SMEM),
            pl.BlockSpec((tq, HEAD_DIM), lambda b, h, q: (b * qb + q, 2 * h)),
            pl.BlockSpec((tq, HEAD_DIM), lambda b, h, q: (b * qb + q, 2 * h + 1)),
            pl.BlockSpec((seq, HEAD_DIM), lambda b, h, q: (b, k_col0 + 2 * h)),
            pl.BlockSpec((seq, HEAD_DIM), lambda b, h, q: (b, k_col0 + 2 * h + 1)),
            pl.BlockSpec((ctx_len, HEAD_DIM), lambda b, h, q: (ctx_row0 + b, k_col0 + 2 * h)),
            pl.BlockSpec((ctx_len, HEAD_DIM), lambda b, h, q: (ctx_row0 + b, k_col0 + 2 * h + 1)),
            pl.BlockSpec((seq, vd), lambda b, h, q: (b, v_col0 + h)),
            pl.BlockSpec((ctx_len, vd), lambda b, h, q: (ctx_row0 + b, v_col0 + h)),
            pl.BlockSpec((1, vd), lambda b, h, q: (0, 0)),
        ],
        out_specs=pl.BlockSpec((tq, vd), lambda b, h, q: (b * qb + q, h)),
        out_shape=jax.ShapeDtypeStruct((t_lat, d), BF16),
        compiler_params=_cparams(56),
        name="diff_attention",
    )(lam, qkv, qkv, qkv, qkv, qkv, qkv, qkv, qkv, subln_g)


def _proj_residual_kernel(a_ref, w_ref, x_ref, gate_ref, o_ref):
    y = jnp.dot(a_ref[...], w_ref[...], preferred_element_type=F32)
    o_ref[...] = x_ref[...] + gate_ref[0] * y


def _proj_residual(a, w_bf16, x_all, gate, ridx, tm=1024, tn=1024):
    t, k = a.shape
    n = w_bf16.shape[1]
    return pl.pallas_call(
        _proj_residual_kernel,
        grid=(t // tm, n // tn),
        in_specs=[
            pl.BlockSpec((tm, k), lambda i, j: (i, 0)),
            pl.BlockSpec((k, tn), lambda i, j: (0, j)),
            pl.BlockSpec((tm, tn), lambda i, j: (i, j)),
            pl.BlockSpec((1, 1, tn), lambda i, j: (ridx(i, tm), 0, j)),
        ],
        out_specs=pl.BlockSpec((tm, tn), lambda i, j: (i, j)),
        out_shape=jax.ShapeDtypeStruct((t, n), F32),
        compiler_params=_cparams(48),
        name="attn_out_proj",
    )(a, w_bf16, x_all, gate)


def _router_kernel(x_ref, g_ref, sh_ref, sc_ref, wr_ref, br_ref, h_ref, idx_ref, wt_ref):
    h = _rms_mod(x_ref[...], g_ref[...], sh_ref[0], sc_ref[0])
    h_ref[...] = h
    logits = lax.dot_general(wr_ref[...], h, (((1,), (1,)), ((), ())),
                             precision=lax.Precision.HIGHEST, preferred_element_type=F32) + br_ref[...]
    n_e, tm = logits.shape
    e_iota = lax.broadcasted_iota(jnp.int32, (n_e, tm), 0).astype(F32)
    vals, idxs = [], []
    for _ in range(TOP_K):
        m = jnp.max(logits, axis=0, keepdims=True)
        idx = jnp.min(jnp.where(logits == m, e_iota, float(n_e)), axis=0, keepdims=True)
        vals.append(m)
        idxs.append(idx)
        logits = jnp.where(e_iota == idx, -jnp.inf, logits)
    exps = [jnp.exp(v - vals[0]) for v in vals]
    denom = exps[0]
    for e in exps[1:]:
        denom = denom + e
    idx_ref[...] = jnp.zeros(idx_ref.shape, jnp.int32)
    wt_ref[...] = jnp.zeros(wt_ref.shape, F32)
    for k in range(TOP_K):
        idx_ref[k:k + 1, :] = idxs[k].astype(jnp.int32)
        wt_ref[k:k + 1, :] = exps[k] / denom


def _router(x_all, t_rows, g, shift, scale, w_r_t, b_r, ridx, tm=512):
    d = x_all.shape[1]
    n_e = w_r_t.shape[0]
    return pl.pallas_call(
        _router_kernel,
        grid=(t_rows // tm,),
        in_specs=[
            pl.BlockSpec((tm, d), lambda i: (i, 0)),
            pl.BlockSpec((1, d), lambda i: (0, 0)),
            pl.BlockSpec((1, 1, d), lambda i: (ridx(i, tm), 0, 0)),
            pl.BlockSpec((1, 1, d), lambda i: (ridx(i, tm), 0, 0)),
            pl.BlockSpec((n_e, d), lambda i: (0, 0)),
            pl.BlockSpec((n_e, 1), lambda i: (0, 0)),
        ],
        out_specs=[
            pl.BlockSpec((tm, d), lambda i: (i, 0)),
            pl.BlockSpec((8, tm), lambda i: (0, i)),
            pl.BlockSpec((8, tm), lambda i: (0, i)),
        ],
        out_shape=[
            jax.ShapeDtypeStruct((t_rows, d), F32),
            jax.ShapeDtypeStruct((8, t_rows), jnp.int32),
            jax.ShapeDtypeStruct((8, t_rows), F32),
        ],
        compiler_params=_cparams(40),
        name="moe_router",
    )(x_all, g, shift, scale, w_r_t, b_r)


PERM_TILE = 256


def _split_gate_up_tile(w_ref, p_ref, og_ref, ol_ref):
    half = PERM_TILE // 2
    tn = w_ref.shape[2]
    for c in range(tn // PERM_TILE):
        w = w_ref[0, :, c * PERM_TILE:(c + 1) * PERM_TILE].astype(BF16)
        t = jnp.dot(w, p_ref[...], preferred_element_type=F32)
        og_ref[0, :, c * half:(c + 1) * half] = t[:, :half].astype(BF16)
        ol_ref[0, :, c * half:(c + 1) * half] = t[:, half:].astype(BF16)


def _split_perm():
    half = PERM_TILE // 2
    col = jnp.arange(PERM_TILE, dtype=jnp.int32)
    src = jnp.where(col < half, 2 * col, 2 * (col - half) + 1)
    return (jnp.arange(PERM_TILE, dtype=jnp.int32)[:, None] == src[None, :]).astype(BF16)


def _split_gate_up(w_gu, n_e, tn=1024):
    _, d, n2 = w_gu.shape
    out = jax.ShapeDtypeStruct((n_e, d, n2 // 2), BF16)
    return pl.pallas_call(
        _split_gate_up_tile,
        grid=(n_e, n2 // tn),
        in_specs=[
            pl.BlockSpec((1, d, tn), lambda e, j: (e, 0, j)),
            pl.BlockSpec((PERM_TILE, PERM_TILE), lambda e, j: (0, 0)),
        ],
        out_specs=[
            pl.BlockSpec((1, d, tn // 2), lambda e, j: (e, 0, j)),
            pl.BlockSpec((1, d, tn // 2), lambda e, j: (e, 0, j)),
        ],
        out_shape=[out, out],
        compiler_params=_cparams(40),
        name="moe_split_gate_up",
    )(w_gu, _split_perm())


SIDE_SPLIT_TILE = 512


def _expert_up_kernel(blk_e_ref, n_used_ref, tok_ref, tok_next_ref, h_ref, wg_ref, wl_ref, bg_ref, bl_ref,
                      *rest, nc, side_split):
    if side_split:
        ws_ref, p_ref, o_ref, sg_ref, sl_ref, xbuf0_ref, xbuf1_ref, sem = rest
    else:
        o_ref, xbuf0_ref, xbuf1_ref, sem = rest
    xbufs = (xbuf0_ref, xbuf1_ref)
    i = pl.program_id(0)
    n_used = n_used_ref[0]

    def row_copy(src_row, r, slot):
        return pltpu.make_async_copy(h_ref.at[pl.ds(src_row, 1)], xbufs[slot].at[pl.ds(r, 1)], sem.at[slot])

    def start_gather(idx_ref, slot):
        for r in range(EXPERT_ROWS):
            row_copy(idx_ref[0, 0, r], r, slot).start()

    def wait_gather(slot):
        for r in range(EXPERT_ROWS):
            row_copy(0, r, slot).wait()

    @pl.when(i == 0)
    def _():
        start_gather(tok_ref, 0)

    def block(slot):
        start_gather(tok_next_ref, 1 - slot)
        wait_gather(slot)
        xb = xbufs[slot][...].astype(BF16)
        de = o_ref.shape[1]
        for c in range(de // nc):
            sl = slice(c * nc, (c + 1) * nc)
            gate = jnp.dot(xb, wg_ref[0, :, sl], preferred_element_type=F32) + bg_ref[0, :, sl]
            lin = jnp.dot(xb, wl_ref[0, :, sl], preferred_element_type=F32) + bl_ref[0, :, sl]
            glu = jnp.minimum(gate, SWIGLU_LIMIT)
            lin = jnp.clip(lin, -SWIGLU_LIMIT, SWIGLU_LIMIT)
            act = glu * jax.nn.sigmoid(SWIGLU_ALPHA * glu) * (lin + 1.0)
            o_ref[:, sl] = act.astype(o_ref.dtype)

        if side_split:
            _split_gate_up_tile(ws_ref, p_ref, sg_ref, sl_ref)

        @pl.when(i + 1 >= n_used)
        def _():
            wait_gather(1 - slot)

    for parity in range(2):
        @pl.when(jnp.logical_and(i < n_used, lax.rem(i, 2) == parity))
        def _():
            block(parity)

    @pl.when(i >= n_used)
    def _():
        o_ref[...] = jnp.zeros(o_ref.shape, o_ref.dtype)


def _expert_up(h, tok_blocks, blk_e, n_used, wg, wl, bg, bl, n_slots, split_next=None, nc=512):
    d = wg.shape[1]
    de = wg.shape[2]
    n_blocks = tok_blocks.shape[0]

    def w_idx(i, be, nu):
        return (be[i], 0, 0)

    in_specs = [
        pl.BlockSpec((1, 1, EXPERT_ROWS), lambda i, be, nu: (i, 0, 0), memory_space=pltpu.SMEM),
        pl.BlockSpec((1, 1, EXPERT_ROWS), lambda i, be, nu: (jnp.minimum(i + 1, n_blocks - 1), 0, 0),
                     memory_space=pltpu.SMEM),
        pl.BlockSpec(memory_space=pl.ANY),
        pl.BlockSpec((1, d, de), w_idx),
        pl.BlockSpec((1, d, de), w_idx),
        pl.BlockSpec((1, 1, de), w_idx),
        pl.BlockSpec((1, 1, de), w_idx),
    ]
    out_specs = [pl.BlockSpec((EXPERT_ROWS, de), lambda i, be, nu: (i, 0))]
    out_shape = [jax.ShapeDtypeStruct((n_blocks * EXPERT_ROWS, de), BF16)]
    operands = [tok_blocks, tok_blocks, h, wg, wl, bg, bl]
    if split_next is not None:
        w_raw, e0, n_e = split_next
        tiles_per_e = w_raw.shape[2] // SIDE_SPLIT_TILE
        n_tiles = n_e * tiles_per_e
        assert n_slots // EXPERT_ROWS >= n_tiles

        def tile_idx(i, be, nu, first):
            t = jnp.minimum(i, n_tiles - 1)
            return (first + t // tiles_per_e, 0, t % tiles_per_e)

        in_specs += [
            pl.BlockSpec((1, d, SIDE_SPLIT_TILE), functools.partial(tile_idx, first=e0)),
            pl.BlockSpec((PERM_TILE, PERM_TILE), lambda i, be, nu: (0, 0)),
        ]
        split_out = jax.ShapeDtypeStruct((n_e, d, w_raw.shape[2] // 2), BF16)
        out_specs += [pl.BlockSpec((1, d, SIDE_SPLIT_TILE // 2), functools.partial(tile_idx, first=0))] * 2
        out_shape += [split_out, split_out]
        operands += [w_raw, _split_perm()]

    grid_spec = pltpu.PrefetchScalarGridSpec(
        num_scalar_prefetch=2,
        grid=(n_blocks,),
        in_specs=in_specs,
        out_specs=out_specs,
        scratch_shapes=[pltpu.VMEM((EXPERT_ROWS, d), h.dtype), pltpu.VMEM((EXPERT_ROWS, d), h.dtype),
                        pltpu.SemaphoreType.DMA((2,))],
    )
    return pl.pallas_call(
        functools.partial(_expert_up_kernel, nc=nc, side_split=split_next is not None),
        grid_spec=grid_spec,
        out_shape=out_shape,
        compiler_params=_cparams(56),
        name="moe_expert_up",
    )(blk_e, n_used, *operands)


def _expert_down_kernel(blk_e_ref, n_used_ref, next_e_ref, parity_ref, a_ref, wd_ref, bd_ref, o_ref,
                        stage0_ref, stage1_ref, wb_ref, sem):
    i = pl.program_id(0)
    new_expert = jnp.logical_or(i == 0, blk_e_ref[i] != blk_e_ref[jnp.maximum(i - 1, 0)])
    stages = (stage0_ref, stage1_ref)

    def weight_copy(expert, slot):
        return pltpu.make_async_copy(wd_ref.at[expert], stages[slot], sem.at[slot])

    @pl.when(i == 0)
    def _():
        weight_copy(blk_e_ref[0], 0).start()

    for parity in range(2):
        @pl.when(jnp.logical_and(jnp.logical_and(i < n_used_ref[0], new_expert), parity_ref[i] == parity))
        def _():
            nxt = next_e_ref[i]

            @pl.when(nxt >= 0)
            def _():
                weight_copy(nxt, 1 - parity).start()

            weight_copy(0, parity).wait()
            wb_ref[...] = stages[parity][...].astype(BF16)

    @pl.when(i < n_used_ref[0])
    def _():
        o_ref[...] = jnp.dot(a_ref[...], wb_ref[...], preferred_element_type=F32) + bd_ref[0]

    @pl.when(i >= n_used_ref[0])
    def _():
        o_ref[...] = jnp.zeros(o_ref.shape, o_ref.dtype)


def _expert_down(act, blk_e, n_used, next_e, parity, wd, bd):
    n_pad, de = act.shape
    d = wd.shape[2]
    n_blocks = n_pad // EXPERT_ROWS

    grid_spec = pltpu.PrefetchScalarGridSpec(
        num_scalar_prefetch=4,
        grid=(n_blocks,),
        in_specs=[
            pl.BlockSpec((EXPERT_ROWS, de), lambda i, be, nu, ne, pa: (jnp.minimum(i, nu[0] - 1), 0)),
            pl.BlockSpec(memory_space=pl.ANY),
            pl.BlockSpec((1, 1, d), lambda i, be, nu, ne, pa: (be[i], 0, 0)),
        ],
        out_specs=pl.BlockSpec((EXPERT_ROWS, d), lambda i, be, nu, ne, pa: (i, 0)),
        scratch_shapes=[pltpu.VMEM((de, d), wd.dtype), pltpu.VMEM((de, d), wd.dtype), pltpu.VMEM((de, d), BF16),
                        pltpu.SemaphoreType.DMA((2,))],
    )
    return pl.pallas_call(
        _expert_down_kernel,
        grid_spec=grid_spec,
        out_shape=jax.ShapeDtypeStruct((n_pad, d), F32),
        compiler_params=_cparams(56),
        name="moe_expert_down",
    )(blk_e, n_used, next_e, parity, act, wd, bd)


def _combine_kernel(dest_ref, dest_next_ref, wt_ref, x_ref, gate_ref, ys_ref, o_ref, buf0_ref, buf1_ref, sem):
    i = pl.program_id(0)
    tm = x_ref.shape[0]
    bufs = (buf0_ref, buf1_ref)

    def row_copy(src_row, k, r, slot):
        return pltpu.make_async_copy(ys_ref.at[pl.ds(src_row, 1)], bufs[slot].at[k, pl.ds(r, 1)], sem.at[slot])

    def start_gather(idx_ref, slot):
        for r in range(tm):
            for k in range(TOP_K):
                row_copy(idx_ref[0, 0, k * tm + r], k, r, slot).start()

    def wait_gather(slot):
        for r in range(tm):
            for k in range(TOP_K):
                row_copy(0, k, r, slot).wait()

    @pl.when(i == 0)
    def _():
        start_gather(dest_ref, 0)

    def block(slot):
        @pl.when(i + 1 < pl.num_programs(0))
        def _():
            start_gather(dest_next_ref, 1 - slot)

        wait_gather(slot)
        acc = wt_ref[:, 0:1] * bufs[slot][0]
        for k in range(1, TOP_K):
            acc = acc + wt_ref[:, k:k + 1] * bufs[slot][k]
        o_ref[...] = x_ref[...] + gate_ref[0] * acc

    for parity in range(2):
        @pl.when(lax.rem(i, 2) == parity)
        def _():
            block(parity)


def _combine(ys, dest_blocks, wt, x_all, t_rows, gate, ridx, tm):
    d = x_all.shape[1]
    n_steps = t_rows // tm
    return pl.pallas_call(
        _combine_kernel,
        grid=(n_steps,),
        in_specs=[
            pl.BlockSpec((1, 1, TOP_K * tm), lambda i: (i, 0, 0), memory_space=pltpu.SMEM),
            pl.BlockSpec((1, 1, TOP_K * tm), lambda i: (jnp.minimum(i + 1, n_steps - 1), 0, 0),
                         memory_space=pltpu.SMEM),
            pl.BlockSpec((tm, TOP_K), lambda i: (i, 0)),
            pl.BlockSpec((tm, d), lambda i: (i, 0)),
            pl.BlockSpec((1, 1, d), lambda i: (ridx(i, tm), 0, 0)),
            pl.BlockSpec(memory_space=pl.ANY),
        ],
        out_specs=pl.BlockSpec((tm, d), lambda i: (i, 0)),
        out_shape=jax.ShapeDtypeStruct((t_rows, d), F32),
        scratch_shapes=[pltpu.VMEM((TOP_K, tm, d), F32), pltpu.VMEM((TOP_K, tm, d), F32),
                        pltpu.SemaphoreType.DMA((2,))],
        compiler_params=_cparams(32),
        name="moe_combine",
    )(dest_blocks, dest_blocks, wt, x_all, gate, ys)


def _slot_tokens_kernel(dest_ref, zeros_ref, o_ref, sem, *, tm):
    i = pl.program_id(0)

    @pl.when(i == 0)
    def _():
        fill = pltpu.make_async_copy(zeros_ref, o_ref, sem)
        fill.start()
        fill.wait()

    def body(r, carry):
        for k in range(TOP_K):
            o_ref[dest_ref[0, 0, k * tm + r]] = i * tm + r
        return carry

    lax.fori_loop(0, tm, body, 0, unroll=8)


def _slot_tokens(dest_blocks, n_blocks, tm):
    n_steps = dest_blocks.shape[0]
    return pl.pallas_call(
        functools.partial(_slot_tokens_kernel, tm=tm),
        grid=(n_steps,),
        in_specs=[
            pl.BlockSpec((1, 1, TOP_K * tm), lambda i: (i, 0, 0), memory_space=pltpu.SMEM),
            pl.BlockSpec(memory_space=pl.ANY),
        ],
        out_specs=pl.BlockSpec(memory_space=pltpu.SMEM),
        out_shape=jax.ShapeDtypeStruct((n_blocks * EXPERT_ROWS,), jnp.int32),
        scratch_shapes=[pltpu.SemaphoreType.DMA],
        name="moe_slot_tokens",
    )(dest_blocks, jnp.zeros((n_blocks * EXPERT_ROWS,), jnp.int32))


def _slot_blocks(dest, tm):
    k, t = dest.shape
    return dest.reshape(k, t // tm, tm).transpose(1, 0, 2).reshape(t // tm, 1, k * tm)


def _moe(x_all, t_rows, norm_g, shift, scale, gate, ridx, w_r, b_r, layer, wg, wl, bg, bl, wd, bd,
         split_next=None, combine_tm=128):
    n_e = w_r.shape[1]
    h, top_idx, top_w = _router(x_all, t_rows, norm_g, shift, scale, w_r.T, b_r.reshape(n_e, 1), ridx)
    top_idx = top_idx[:TOP_K]
    top_w = top_w[:TOP_K]

    n_slots = TOP_K * t_rows
    e_flat = top_idx.reshape(n_slots)
    onehot = (e_flat[:, None] == jnp.arange(n_e, dtype=jnp.int32)[None, :]).astype(jnp.int32)
    csum = jnp.cumsum(onehot, axis=0)
    rank = jnp.sum(onehot * csum, axis=1) - 1
    counts = csum[-1]
    padded = (counts + EXPERT_ROWS - 1) // EXPERT_ROWS * EXPERT_ROWS
    pad_ends = jnp.cumsum(padded)
    pad_starts = pad_ends - padded
    dest = (jnp.sum(onehot * pad_starts[None, :], axis=1) + rank).reshape(TOP_K, t_rows)
    n_blocks = -(-(n_slots + n_e * (EXPERT_ROWS - 1)) // EXPERT_ROWS)
    n_pad = n_blocks * EXPERT_ROWS
    blk_row0 = jnp.arange(n_blocks, dtype=jnp.int32) * EXPERT_ROWS
    blk_e = jnp.minimum(jnp.sum((pad_ends[None, :] <= blk_row0[:, None]).astype(jnp.int32), axis=1), n_e - 1)
    n_used = (pad_ends[-1:] // EXPERT_ROWS).astype(jnp.int32)
    tok_blocks = _slot_tokens(_slot_blocks(dest, 512), n_blocks, 512).reshape(n_blocks, 1, EXPERT_ROWS)
    blk_e = blk_e.astype(jnp.int32)
    blk_i = jnp.arange(n_blocks, dtype=jnp.int32)
    first_of_expert = jnp.logical_or(blk_i == 0, blk_e != jnp.roll(blk_e, 1))
    parity = ((jnp.cumsum(first_of_expert.astype(jnp.int32)) - 1) % 2).astype(jnp.int32)
    next_blk = (pad_ends // EXPERT_ROWS).astype(jnp.int32)[blk_e]
    next_e = jnp.where(next_blk < n_used[0], blk_e[jnp.minimum(next_blk, n_blocks - 1)] + layer * n_e, -1)

    up = _expert_up(h, tok_blocks, blk_e, n_used, wg, wl, bg, bl, n_slots, split_next)
    ys = _expert_down(up[0], blk_e + layer * n_e, n_used, next_e.astype(jnp.int32), parity, wd, bd)
    out = _combine(ys, _slot_blocks(dest, combine_tm), top_w.T, x_all, t_rows, gate, ridx, combine_tm)
    return out, tuple(up[1:])


def _rope_tables(seq, pad_rows):
    rows = seq // GRID_W
    row_pos = jnp.repeat(jnp.arange(rows), GRID_W).astype(F32)
    col_pos = jnp.tile(jnp.arange(GRID_W), rows).astype(F32)
    n_freq = HEAD_DIM // 4
    inv_freq = ROPE_THETA ** (-jnp.arange(n_freq, dtype=F32) / n_freq)
    ang_r = row_pos[:, None] * inv_freq
    ang_c = col_pos[:, None] * inv_freq
    cos = jnp.concatenate([jnp.cos(ang_r)] * 2 + [jnp.cos(ang_c)] * 2, axis=1)
    sin = jnp.concatenate([-jnp.sin(ang_r), jnp.sin(ang_r), -jnp.sin(ang_c), jnp.sin(ang_c)], axis=1)
    cos = jnp.concatenate([cos, jnp.ones((pad_rows, HEAD_DIM), F32)], axis=0)
    sin = jnp.concatenate([sin, jnp.zeros((pad_rows, HEAD_DIM), F32)], axis=0)
    return cos, sin


def kernel(x, c, ctx, c_ctx, w_mod, b_mod, norm_mix_g, norm_ffn_g, gmlp_w_in, gmlp_norm_g, gmlp_w_s, gmlp_b_s,
           gmlp_w_out, diff_w_qkv, diff_q_norm_g, diff_k_norm_g, diff_lambda, diff_subln_g, diff_w_o,
           moe_w_router, moe_b_router, moe_w_gate_up, moe_b_gate_up, moe_w_down, moe_b_down):
    batch, seq, d = x.shape
    ctx_len = ctx.shape[1]
    depth = w_mod.shape[0]
    t_lat = batch * seq
    t_ctx = batch * ctx_len
    t_all = t_lat + t_ctx
    assert batch < MOD_ROWS and depth == 2

    def ridx(i, tm):
        return jnp.minimum((i * tm) // seq, batch)

    c16 = jnp.concatenate([c, c_ctx[None, :], jnp.zeros((MOD_ROWS - batch - 1, d), F32)], axis=0)
    mods = _modulation(c16, w_mod, b_mod).reshape(depth, MOD_ROWS, N_MOD, d)

    def mod(layer, k):
        return mods[layer, :, k, :].reshape(MOD_ROWS, 1, d)

    x_all = jnp.concatenate([x.reshape(t_lat, d), ctx.reshape(t_ctx, d)], axis=0)

    n_e, de2 = moe_w_gate_up.shape[1], moe_w_gate_up.shape[3]
    de = de2 // 2
    w_gu_all = moe_w_gate_up.reshape(depth * n_e, d, de2)
    wg0, wl0 = _split_gate_up(w_gu_all, n_e)
    bg = moe_b_gate_up[:, :, 0::2].reshape(depth, n_e, 1, de)
    bl = moe_b_gate_up[:, :, 1::2].reshape(depth, n_e, 1, de)
    wd = moe_w_down.reshape(depth * n_e, de, d)
    bd = moe_b_down.reshape(depth * n_e, 1, d)

    uv = _gmlp_in(x_all, norm_mix_g[0:1], mod(0, 0), mod(0, 1), gmlp_w_in[0].astype(BF16), ridx)
    x_all = _gmlp_out(uv, gmlp_norm_g[0:1], gmlp_w_s[0].astype(BF16), gmlp_b_s[0].T,
                      gmlp_w_out[0].astype(BF16), x_all, mod(0, 2), ridx)
    x_all, (wg1, wl1) = _moe(x_all, t_all, norm_ffn_g[0:1], mod(0, 3), mod(0, 4), mod(0, 5), ridx,
                             moe_w_router[0], moe_b_router[0], 0, wg0, wl0, bg[0], bl[0], wd, bd,
                             split_next=(w_gu_all, n_e, n_e))

    lam_init = 0.8 - 0.6 * math.exp(-0.3 * 1)
    lf = diff_lambda[0].astype(F32)
    lam = (jnp.exp(jnp.sum(lf[0] * lf[1])) - jnp.exp(jnp.sum(lf[2] * lf[3])) + lam_init).reshape(1)
    tm_qkv = 1024
    cos_t, sin_t = _rope_tables(seq, tm_qkv)
    n_heads2 = d // HEAD_DIM
    q_scale = HEAD_DIM ** -0.5 * math.log2(math.e)
    head_gain = jnp.concatenate([jnp.tile(diff_q_norm_g[0] * q_scale, n_heads2),
                                 jnp.tile(diff_k_norm_g[0], n_heads2),
                                 jnp.ones((d,), F32)]).reshape(1, 3 * d)
    qkv = _qkv(x_all, t_all, t_lat, norm_mix_g[1:2], mod(1, 0), mod(1, 1), diff_w_qkv[0].astype(BF16), head_gain,
               cos_t, sin_t, ridx, seq, tm=tm_qkv)
    attn = _diff_attention(qkv, lam, diff_subln_g[0:1], batch, seq, ctx_len, d, 1.0 - lam_init)
    x_lat = _proj_residual(attn, diff_w_o[0].astype(BF16), x_all, mod(1, 2), ridx)
    x_lat, _ = _moe(x_lat, t_lat, norm_ffn_g[1:2], mod(1, 3), mod(1, 4), mod(1, 5), ridx,
                    moe_w_router[1], moe_b_router[1], 1, wg1, wl1, bg[1], bl[1], wd, bd)
    return x_lat.reshape(batch, seq, d)
```

```python
import functools
import math

import jax
import jax.numpy as jnp
from jax import lax
from jax.experimental import pallas as pl
from jax.experimental.pallas import tpu as pltpu

F32 = jnp.float32
BF16 = jnp.bfloat16

NORM_EPS = 1e-6
N_MOD = 6
GRID_W = 64
GMLP_CHUNK = 128
GMLP_GROUPS = 8
DIFF_HEADS = 8
HEAD_DIM = 128
ROPE_THETA = 10000.0
N_EXPERTS = 32
TOP_K = 4
SWIGLU_ALPHA = 1.702
SWIGLU_LIMIT = 7.0

MOD_ROWS = 16
EXPERT_ROWS = 256
MIB = 1024 * 1024


def _cparams(vmem_mib):
    return pltpu.CompilerParams(vmem_limit_bytes=vmem_mib * MIB)


def _rms_mod(x, g, shift, scale):
    ms = jnp.mean(x * x, axis=-1, keepdims=True)
    y = x * lax.rsqrt(ms + NORM_EPS) * g
    return y * (1.0 + scale) + shift


def _mod_kernel(c_ref, w_ref, b_ref, o_ref):
    c = c_ref[...]
    s = c * jax.nn.sigmoid(c)
    o_ref[0] = jnp.dot(s.astype(BF16), w_ref[0].astype(BF16), preferred_element_type=F32) + b_ref[0]


def _modulation(c16, w_mod, b_mod):
    depth, d, n = w_mod.shape
    tn = 1024
    return pl.pallas_call(
        _mod_kernel,
        grid=(depth, n // tn),
        in_specs=[
            pl.BlockSpec((MOD_ROWS, d), lambda l, j: (0, 0)),
            pl.BlockSpec((1, d, tn), lambda l, j: (l, 0, j)),
            pl.BlockSpec((1, 1, tn), lambda l, j: (l, 0, j)),
        ],
        out_specs=pl.BlockSpec((1, MOD_ROWS, tn), lambda l, j: (l, 0, j)),
        out_shape=jax.ShapeDtypeStruct((depth, MOD_ROWS, n), F32),
        compiler_params=_cparams(40),
        name="modulation",
    )(c16, w_mod, b_mod.reshape(depth, 1, n))


def _gmlp_in_kernel(x_ref, g_ref, sh_ref, sc_ref, w_ref, o_ref, h_ref):
    @pl.when(pl.program_id(1) == 0)
    def _():
        h_ref[...] = _rms_mod(x_ref[...], g_ref[...], sh_ref[0], sc_ref[0]).astype(BF16)

    acc = jnp.dot(h_ref[...], w_ref[...], preferred_element_type=F32)
    gelu = 0.5 * acc * (1.0 + lax.erf(acc * (2.0 ** -0.5)))
    o_ref[...] = gelu.astype(o_ref.dtype)


def _gmlp_in(x, g, shift, scale, w_bf16, ridx, tm=1024, tn=1024):
    t, d = x.shape
    n = w_bf16.shape[1]
    return pl.pallas_call(
        _gmlp_in_kernel,
        grid=(t // tm, n // tn),
        in_specs=[
            pl.BlockSpec((tm, d), lambda i, j: (i, 0)),
            pl.BlockSpec((1, d), lambda i, j: (0, 0)),
            pl.BlockSpec((1, 1, d), lambda i, j: (ridx(i, tm), 0, 0)),
            pl.BlockSpec((1, 1, d), lambda i, j: (ridx(i, tm), 0, 0)),
            pl.BlockSpec((d, tn), lambda i, j: (0, j)),
        ],
        out_specs=pl.BlockSpec((tm, tn), lambda i, j: (i, j)),
        out_shape=jax.ShapeDtypeStruct((t, n), BF16),
        scratch_shapes=[pltpu.VMEM((tm, d), BF16)],
        compiler_params=_cparams(48),
        name="gmlp_in",
    )(x, g, shift, scale, w_bf16)


def _gmlp_out_kernel(u_ref, v_ref, lng_ref, ws_ref, bs_ref, wo_ref, x_ref, gate_ref, o_ref, t_ref):
    tm, width = v_ref.shape
    gw = width // GMLP_GROUPS
    v = v_ref[...].astype(F32)
    mu = jnp.mean(v, axis=-1, keepdims=True)
    vc = v - mu
    var = jnp.mean(vc * vc, axis=-1, keepdims=True)
    vn = (vc * lax.rsqrt(var + NORM_EPS) * lng_ref[...]).astype(BF16)
    for c in range(tm // GMLP_CHUNK):
        r0 = c * GMLP_CHUNK
        for g in range(GMLP_GROUPS):
            c0 = g * gw
            mixed = jnp.dot(ws_ref[g], vn[r0:r0 + GMLP_CHUNK, c0:c0 + gw], preferred_element_type=F32)
            mixed = mixed + bs_ref[:, g:g + 1]
            u = u_ref[r0:r0 + GMLP_CHUNK, c0:c0 + gw].astype(F32)
            t_ref[r0:r0 + GMLP_CHUNK, c0:c0 + gw] = (u * mixed).astype(BF16)
    y = jnp.dot(t_ref[...], wo_ref[...], preferred_element_type=F32)
    o_ref[...] = x_ref[...] + gate_ref[0] * y


def _gmlp_out(uv, ln_g, w_s_bf16, b_s_t, w_out_bf16, x, gate, ridx, tm=512):
    t, d = x.shape
    width = uv.shape[1] // 2
    return pl.pallas_call(
        _gmlp_out_kernel,
        grid=(t // tm,),
        in_specs=[
            pl.BlockSpec((tm, width), lambda i: (i, 0)),
            pl.BlockSpec((tm, width), lambda i: (i, 1)),
            pl.BlockSpec((1, width), lambda i: (0, 0)),
            pl.BlockSpec(w_s_bf16.shape, lambda i: (0, 0, 0)),
            pl.BlockSpec(b_s_t.shape, lambda i: (0, 0)),
            pl.BlockSpec((width, d), lambda i: (0, 0)),
            pl.BlockSpec((tm, d), lambda i: (i, 0)),
            pl.BlockSpec((1, 1, d), lambda i: (ridx(i, tm), 0, 0)),
        ],
        out_specs=pl.BlockSpec((tm, d), lambda i: (i, 0)),
        out_shape=jax.ShapeDtypeStruct((t, d), F32),
        scratch_shapes=[pltpu.VMEM((tm, width), BF16)],
        compiler_params=_cparams(56),
        name="gmlp_out",
    )(uv, uv, ln_g, w_s_bf16, b_s_t, w_out_bf16, x, gate)


def _qkv_kernel(x_ref, g_ref, sh_ref, sc_ref, w_ref, hg_ref, cos_ref, sin_ref, o_ref, h_ref, *, n_qk_tiles):
    j = pl.program_id(1)

    @pl.when(j == 0)
    def _():
        h_ref[...] = _rms_mod(x_ref[...], g_ref[...], sh_ref[0], sc_ref[0]).astype(BF16)

    acc = jnp.dot(h_ref[...], w_ref[...], preferred_element_type=F32)
    tm, tn = acc.shape

    @pl.when(j < n_qk_tiles)
    def _():
        cos = cos_ref[...]
        sin = sin_ref[...]
        first_half = (lax.broadcasted_iota(jnp.int32, (tm, HEAD_DIM), 1) % (HEAD_DIM // 2)) < (HEAD_DIM // 4)
        for hh in range(tn // HEAD_DIM):
            a = acc[:, hh * HEAD_DIM:(hh + 1) * HEAD_DIM]
            ms = jnp.mean(a * a, axis=-1, keepdims=True)
            a = a * lax.rsqrt(ms + NORM_EPS) * hg_ref[:, hh * HEAD_DIM:(hh + 1) * HEAD_DIM]
            partner = jnp.where(first_half,
                                pltpu.roll(a, HEAD_DIM - HEAD_DIM // 4, 1),
                                pltpu.roll(a, HEAD_DIM // 4, 1))
            o_ref[:, hh * HEAD_DIM:(hh + 1) * HEAD_DIM] = (a * cos + partner * sin).astype(o_ref.dtype)

    @pl.when(j >= n_qk_tiles)
    def _():
        o_ref[...] = acc.astype(o_ref.dtype)


def _qkv(x_all, t_rows, t_lat, g, shift, scale, w_bf16, head_gain, cos_t, sin_t, ridx, seq, tm=1024, tn=1024):
    d = x_all.shape[1]
    n = w_bf16.shape[1]
    n_qk_tiles = (2 * n // 3) // tn
    seq_blocks = seq // tm
    n_lat_blocks = t_lat // tm

    def rope_idx(i, j):
        return (jnp.where(i < n_lat_blocks, i % seq_blocks, seq_blocks), 0)

    return pl.pallas_call(
        functools.partial(_qkv_kernel, n_qk_tiles=n_qk_tiles),
        grid=(t_rows // tm, n // tn),
        in_specs=[
            pl.BlockSpec((tm, d), lambda i, j: (i, 0)),
            pl.BlockSpec((1, d), lambda i, j: (0, 0)),
            pl.BlockSpec((1, 1, d), lambda i, j: (ridx(i, tm), 0, 0)),
            pl.BlockSpec((1, 1, d), lambda i, j: (ridx(i, tm), 0, 0)),
            pl.BlockSpec((d, tn), lambda i, j: (0, j)),
            pl.BlockSpec((1, tn), lambda i, j: (0, j)),
            pl.BlockSpec((tm, HEAD_DIM), rope_idx),
            pl.BlockSpec((tm, HEAD_DIM), rope_idx),
        ],
        out_specs=pl.BlockSpec((tm, tn), lambda i, j: (i, j)),
        out_shape=jax.ShapeDtypeStruct((t_rows, n), BF16),
        scratch_shapes=[pltpu.VMEM((tm, d), BF16)],
        compiler_params=_cparams(48),
        name="qkv_proj",
    )(x_all, g, shift, scale, w_bf16, head_gain, cos_t, sin_t)


def _attn_kernel(lam_ref, q1_ref, q2_ref, k1l_ref, k2l_ref, k1c_ref, k2c_ref, vl_ref, vc_ref, sg_ref, o_ref,
                 *, out_scale, key_chunk):
    lam = lam_ref[0]
    nt = (((1,), (1,)), ((), ()))

    def softmax_times_v(q_ref, kc_ref, kl_ref):
        q = q_ref[...]
        tq = q.shape[0]
        chunks = [(kc_ref, vc_ref, c0, min(key_chunk, kc_ref.shape[0] - c0))
                  for c0 in range(0, kc_ref.shape[0], key_chunk)]
        chunks += [(kl_ref, vl_ref, c0, min(key_chunk, kl_ref.shape[0] - c0))
                   for c0 in range(0, kl_ref.shape[0], key_chunk)]
        m = jnp.full((tq, 1), -jnp.inf, F32)
        z = jnp.zeros((tq, 1), F32)
        acc = jnp.zeros((tq, vl_ref.shape[1]), F32)
        for k_ref, v_ref, c0, n in chunks:
            s = lax.dot_general(q, k_ref[c0:c0 + n, :], nt, preferred_element_type=F32)
            m_new = jnp.maximum(m, jnp.max(s, axis=-1, keepdims=True))
            alpha = jnp.exp2(m - m_new)
            p = jnp.exp2(s - m_new)
            z = alpha * z + jnp.sum(p, axis=-1, keepdims=True)
            acc = alpha * acc + jnp.dot(p.astype(BF16), v_ref[c0:c0 + n, :], preferred_element_type=F32)
            m = m_new
        return acc / z

    o = softmax_times_v(q1_ref, k1c_ref, k1l_ref) - lam * softmax_times_v(q2_ref, k2c_ref, k2l_ref)
    ms = jnp.mean(o * o, axis=-1, keepdims=True)
    o = o * lax.rsqrt(ms + NORM_EPS) * sg_ref[...] * out_scale
    o_ref[...] = o.astype(o_ref.dtype)


def _diff_attention(qkv, lam, subln_g, batch, seq, ctx_len, d, out_scale, tq=512, key_chunk=512):
    t_lat = batch * seq
    vd = 2 * HEAD_DIM
    qb = seq // tq
    k_col0 = d // HEAD_DIM
    v_col0 = 2 * d // vd
    ctx_row0 = t_lat // ctx_len

    return pl.pallas_call(
        functools.partial(_attn_kernel, out_scale=out_scale, key_chunk=key_chunk),
        grid=(batch, DIFF_HEADS, qb),
        in_specs=[
            pl.BlockSpec(memory_space=pltpu.SMEM),
            pl.BlockSpec((tq, HEAD_DIM), lambda b, h, q: (b * qb + q, 2 * h)),
            pl.BlockSpec((tq, HEAD_DIM), lambda b, h, q: (b * qb + q, 2 * h + 1)),
            pl.BlockSpec((seq, HEAD_DIM), lambda b, h, q: (b, k_col0 + 2 * h)),
            pl.BlockSpec((seq, HEAD_DIM), lambda b, h, q: (b, k_col0 + 2 * h + 1)),
            pl.BlockSpec((ctx_len, HEAD_DIM), lambda b, h, q: (ctx_row0 + b, k_col0 + 2 * h)),
            pl.BlockSpec((ctx_len, HEAD_DIM), lambda b, h, q: (ctx_row0 + b, k_col0 + 2 * h + 1)),
            pl.BlockSpec((seq, vd), lambda b, h, q: (b, v_col0 + h)),
            pl.BlockSpec((ctx_len, vd), lambda b, h, q: (ctx_row0 + b, v_col0 + h)),
            pl.BlockSpec((1, vd), lambda b, h, q: (0, 0)),
        ],
        out_specs=pl.BlockSpec((tq, vd), lambda b, h, q: (b * qb + q, h)),
        out_shape=jax.ShapeDtypeStruct((t_lat, d), BF16),
        compiler_params=_cparams(56),
        name="diff_attention",
    )(lam, qkv, qkv, qkv, qkv, qkv, qkv, qkv, qkv, subln_g)


def _proj_residual_kernel(a_ref, w_ref, x_ref, gate_ref, o_ref):
    y = jnp.dot(a_ref[...], w_ref[...], preferred_element_type=F32)
    o_ref[...] = x_ref[...] + gate_ref[0] * y


def _proj_residual(a, w_bf16, x_all, gate, ridx, tm=1024, tn=1024):
    t, k = a.shape
    n = w_bf16.shape[1]
    return pl.pallas_call(
        _proj_residual_kernel,
        grid=(t // tm, n // tn),
        in_specs=[
            pl.BlockSpec((tm, k), lambda i, j: (i, 0)),
            pl.BlockSpec((k, tn), lambda i, j: (0, j)),
            pl.BlockSpec((tm, tn), lambda i, j: (i, j)),
            pl.BlockSpec((1, 1, tn), lambda i, j: (ridx(i, tm), 0, j)),
        ],
        out_specs=pl.BlockSpec((tm, tn), lambda i, j: (i, j)),
        out_shape=jax.ShapeDtypeStruct((t, n), F32),
        compiler_params=_cparams(48),
        name="attn_out_proj",
    )(a, w_bf16, x_all, gate)


def _router_kernel(x_ref, g_ref, sh_ref, sc_ref, wr_ref, br_ref, h_ref, idx_ref, wt_ref):
    h = _rms_mod(x_ref[...], g_ref[...], sh_ref[0], sc_ref[0])
    h_ref[...] = h
    logits = lax.dot_general(wr_ref[...], h, (((1,), (1,)), ((), ())),
                             precision=lax.Precision.HIGHEST, preferred_element_type=F32) + br_ref[...]
    n_e, tm = logits.shape
    e_iota = lax.broadcasted_iota(jnp.int32, (n_e, tm), 0).astype(F32)
    vals, idxs = [], []
    for _ in range(TOP_K):
        m = jnp.max(logits, axis=0, keepdims=True)
        idx = jnp.min(jnp.where(logits == m, e_iota, float(n_e)), axis=0, keepdims=True)
        vals.append(m)
        idxs.append(idx)
        logits = jnp.where(e_iota == idx, -jnp.inf, logits)
    exps = [jnp.exp(v - vals[0]) for v in vals]
    denom = exps[0]
    for e in exps[1:]:
        denom = denom + e
    idx_ref[...] = jnp.zeros(idx_ref.shape, jnp.int32)
    wt_ref[...] = jnp.zeros(wt_ref.shape, F32)
    for k in range(TOP_K):
        idx_ref[k:k + 1, :] = idxs[k].astype(jnp.int32)
        wt_ref[k:k + 1, :] = exps[k] / denom


def _router(x_all, t_rows, g, shift, scale, w_r_t, b_r, ridx, tm=512):
    d = x_all.shape[1]
    n_e = w_r_t.shape[0]
    return pl.pallas_call(
        _router_kernel,
        grid=(t_rows // tm,),
        in_specs=[
            pl.BlockSpec((tm, d), lambda i: (i, 0)),
            pl.BlockSpec((1, d), lambda i: (0, 0)),
            pl.BlockSpec((1, 1, d), lambda i: (ridx(i, tm), 0, 0)),
            pl.BlockSpec((1, 1, d), lambda i: (ridx(i, tm), 0, 0)),
            pl.BlockSpec((n_e, d), lambda i: (0, 0)),
            pl.BlockSpec((n_e, 1), lambda i: (0, 0)),
        ],
        out_specs=[
            pl.BlockSpec((tm, d), lambda i: (i, 0)),
            pl.BlockSpec((8, tm), lambda i: (0, i)),
            pl.BlockSpec((8, tm), lambda i: (0, i)),
        ],
        out_shape=[
            jax.ShapeDtypeStruct((t_rows, d), F32),
            jax.ShapeDtypeStruct((8, t_rows), jnp.int32),
            jax.ShapeDtypeStruct((8, t_rows), F32),
        ],
        compiler_params=_cparams(40),
        name="moe_router",
    )(x_all, g, shift, scale, w_r_t, b_r)


PERM_TILE = 256


def _split_gate_up_tile(w_ref, p_ref, og_ref, ol_ref):
    half = PERM_TILE // 2
    tn = w_ref.shape[2]
    for c in range(tn // PERM_TILE):
        w = w_ref[0, :, c * PERM_TILE:(c + 1) * PERM_TILE].astype(BF16)
        t = jnp.dot(w, p_ref[...], preferred_element_type=F32)
        og_ref[0, :, c * half:(c + 1) * half] = t[:, :half].astype(BF16)
        ol_ref[0, :, c * half:(c + 1) * half] = t[:, half:].astype(BF16)


def _split_perm():
    half = PERM_TILE // 2
    col = jnp.arange(PERM_TILE, dtype=jnp.int32)
    src = jnp.where(col < half, 2 * col, 2 * (col - half) + 1)
    return (jnp.arange(PERM_TILE, dtype=jnp.int32)[:, None] == src[None, :]).astype(BF16)


def _split_gate_up(w_gu, n_e, tn=1024):
    _, d, n2 = w_gu.shape
    out = jax.ShapeDtypeStruct((n_e, d, n2 // 2), BF16)
    return pl.pallas_call(
        _split_gate_up_tile,
        grid=(n_e, n2 // tn),
        in_specs=[
            pl.BlockSpec((1, d, tn), lambda e, j: (e, 0, j)),
            pl.BlockSpec((PERM_TILE, PERM_TILE), lambda e, j: (0, 0)),
        ],
        out_specs=[
            pl.BlockSpec((1, d, tn // 2), lambda e, j: (e, 0, j)),
            pl.BlockSpec((1, d, tn // 2), lambda e, j: (e, 0, j)),
        ],
        out_shape=[out, out],
        compiler_params=_cparams(40),
        name="moe_split_gate_up",
    )(w_gu, _split_perm())


SIDE_SPLIT_TILE = 512


def _expert_up_kernel(blk_e_ref, n_used_ref, tok_ref, tok_next_ref, h_ref, wg_ref, wl_ref, bg_ref, bl_ref,
                      *rest, nc, side_split):
    if side_split:
        ws_ref, p_ref, o_ref, sg_ref, sl_ref, xbuf0_ref, xbuf1_ref, sem = rest
    else:
        o_ref, xbuf0_ref, xbuf1_ref, sem = rest
    xbufs = (xbuf0_ref, xbuf1_ref)
    i = pl.program_id(0)
    n_used = n_used_ref[0]

    def row_copy(src_row, r, slot):
        return pltpu.make_async_copy(h_ref.at[pl.ds(src_row, 1)], xbufs[slot].at[pl.ds(r, 1)], sem.at[slot])

    def start_gather(idx_ref, slot):
        for r in range(EXPERT_ROWS):
            row_copy(idx_ref[0, 0, r], r, slot).start()

    def wait_gather(slot):
        for r in range(EXPERT_ROWS):
            row_copy(0, r, slot).wait()

    @pl.when(i == 0)
    def _():
        start_gather(tok_ref, 0)

    def block(slot):
        wait_gather(slot)
        start_gather(tok_next_ref, 1 - slot)
        xb = xbufs[slot][...].astype(BF16)
        de = o_ref.shape[1]
        for c in range(de // nc):
            sl = slice(c * nc, (c + 1) * nc)
            gate = jnp.dot(xb, wg_ref[0, :, sl], preferred_element_type=F32) + bg_ref[0, :, sl]
            lin = jnp.dot(xb, wl_ref[0, :, sl], preferred_element_type=F32) + bl_ref[0, :, sl]
            glu = jnp.minimum(gate, SWIGLU_LIMIT)
            lin = jnp.clip(lin, -SWIGLU_LIMIT, SWIGLU_LIMIT)
            act = glu * jax.nn.sigmoid(SWIGLU_ALPHA * glu) * (lin + 1.0)
            o_ref[:, sl] = act.astype(o_ref.dtype)

        if side_split:
            _split_gate_up_tile(ws_ref, p_ref, sg_ref, sl_ref)

        @pl.when(i + 1 >= n_used)
        def _():
            wait_gather(1 - slot)

    for parity in range(2):
        @pl.when(jnp.logical_and(i < n_used, lax.rem(i, 2) == parity))
        def _():
            block(parity)

    @pl.when(i >= n_used)
    def _():
        o_ref[...] = jnp.zeros(o_ref.shape, o_ref.dtype)


def _expert_up(h, tok_blocks, blk_e, n_used, wg, wl, bg, bl, n_slots, split_next=None, nc=512):
    d = wg.shape[1]
    de = wg.shape[2]
    n_blocks = tok_blocks.shape[0]

    def w_idx(i, be, nu):
        return (be[i], 0, 0)

    in_specs = [
        pl.BlockSpec((1, 1, EXPERT_ROWS), lambda i, be, nu: (i, 0, 0), memory_space=pltpu.SMEM),
        pl.BlockSpec((1, 1, EXPERT_ROWS), lambda i, be, nu: (jnp.minimum(i + 1, n_blocks - 1), 0, 0),
                     memory_space=pltpu.SMEM),
        pl.BlockSpec(memory_space=pl.ANY),
        pl.BlockSpec((1, d, de), w_idx),
        pl.BlockSpec((1, d, de), w_idx),
        pl.BlockSpec((1, 1, de), w_idx),
        pl.BlockSpec((1, 1, de), w_idx),
    ]
    out_specs = [pl.BlockSpec((EXPERT_ROWS, de), lambda i, be, nu: (i, 0))]
    out_shape = [jax.ShapeDtypeStruct((n_blocks * EXPERT_ROWS, de), BF16)]
    operands = [tok_blocks, tok_blocks, h, wg, wl, bg, bl]
    if split_next is not None:
        w_raw, e0, n_e = split_next
        tiles_per_e = w_raw.shape[2] // SIDE_SPLIT_TILE
        n_tiles = n_e * tiles_per_e
        assert n_slots // EXPERT_ROWS >= n_tiles

        def tile_idx(i, be, nu, first):
            t = jnp.minimum(i, n_tiles - 1)
            return (first + t // tiles_per_e, 0, t % tiles_per_e)

        in_specs += [
            pl.BlockSpec((1, d, SIDE_SPLIT_TILE), functools.partial(tile_idx, first=e0)),
            pl.BlockSpec((PERM_TILE, PERM_TILE), lambda i, be, nu: (0, 0)),
        ]
        split_out = jax.ShapeDtypeStruct((n_e, d, w_raw.shape[2] // 2), BF16)
        out_specs += [pl.BlockSpec((1, d, SIDE_SPLIT_TILE // 2), functools.partial(tile_idx, first=0))] * 2
        out_shape += [split_out, split_out]
        operands += [w_raw, _split_perm()]

    grid_spec = pltpu.PrefetchScalarGridSpec(
        num_scalar_prefetch=2,
        grid=(n_blocks,),
        in_specs=in_specs,
        out_specs=out_specs,
        scratch_shapes=[pltpu.VMEM((EXPERT_ROWS, d), h.dtype), pltpu.VMEM((EXPERT_ROWS, d), h.dtype),
                        pltpu.SemaphoreType.DMA((2,))],
    )
    return pl.pallas_call(
        functools.partial(_expert_up_kernel, nc=nc, side_split=split_next is not None),
        grid_spec=grid_spec,
        out_shape=out_shape,
        compiler_params=_cparams(56),
        name="moe_expert_up",
    )(blk_e, n_used, *operands)


def _expert_down_kernel(blk_e_ref, n_used_ref, next_e_ref, parity_ref, a_ref, wd_ref, bd_ref, o_ref,
                        stage0_ref, stage1_ref, wb_ref, sem):
    i = pl.program_id(0)
    new_expert = jnp.logical_or(i == 0, blk_e_ref[i] != blk_e_ref[jnp.maximum(i - 1, 0)])
    stages = (stage0_ref, stage1_ref)

    def weight_copy(expert, slot):
        return pltpu.make_async_copy(wd_ref.at[expert], stages[slot], sem.at[slot])

    @pl.when(i == 0)
    def _():
        weight_copy(blk_e_ref[0], 0).start()

    for parity in range(2):
        @pl.when(jnp.logical_and(jnp.logical_and(i < n_used_ref[0], new_expert), parity_ref[i] == parity))
        def _():
            nxt = next_e_ref[i]

            @pl.when(nxt >= 0)
            def _():
                weight_copy(nxt, 1 - parity).start()

            weight_copy(0, parity).wait()
            wb_ref[...] = stages[parity][...].astype(BF16)

    @pl.when(i < n_used_ref[0])
    def _():
        o_ref[...] = jnp.dot(a_ref[...], wb_ref[...], preferred_element_type=F32) + bd_ref[0]

    @pl.when(i >= n_used_ref[0])
    def _():
        o_ref[...] = jnp.zeros(o_ref.shape, o_ref.dtype)


def _expert_down(act, blk_e, n_used, next_e, parity, wd, bd):
    n_pad, de = act.shape
    d = wd.shape[2]
    n_blocks = n_pad // EXPERT_ROWS

    grid_spec = pltpu.PrefetchScalarGridSpec(
        num_scalar_prefetch=4,
        grid=(n_blocks,),
        in_specs=[
            pl.BlockSpec((EXPERT_ROWS, de), lambda i, be, nu, ne, pa: (jnp.minimum(i, nu[0] - 1), 0)),
            pl.BlockSpec(memory_space=pl.ANY),
            pl.BlockSpec((1, 1, d), lambda i, be, nu, ne, pa: (be[i], 0, 0)),
        ],
        out_specs=pl.BlockSpec((EXPERT_ROWS, d), lambda i, be, nu, ne, pa: (i, 0)),
        scratch_shapes=[pltpu.VMEM((de, d), wd.dtype), pltpu.VMEM((de, d), wd.dtype), pltpu.VMEM((de, d), BF16),
                        pltpu.SemaphoreType.DMA((2,))],
    )
    return pl.pallas_call(
        _expert_down_kernel,
        grid_spec=grid_spec,
        out_shape=jax.ShapeDtypeStruct((n_pad, d), F32),
        compiler_params=_cparams(56),
        name="moe_expert_down",
    )(blk_e, n_used, next_e, parity, act, wd, bd)


def _combine_kernel(dest_ref, dest_next_ref, wt_ref, x_ref, gate_ref, ys_ref, o_ref, buf0_ref, buf1_ref, sem):
    i = pl.program_id(0)
    tm = x_ref.shape[0]
    bufs = (buf0_ref, buf1_ref)

    def row_copy(src_row, k, r, slot):
        return pltpu.make_async_copy(ys_ref.at[pl.ds(src_row, 1)], bufs[slot].at[k, pl.ds(r, 1)], sem.at[slot])

    def start_gather(idx_ref, slot):
        for r in range(tm):
            for k in range(TOP_K):
                row_copy(idx_ref[0, 0, k * tm + r], k, r, slot).start()

    def wait_gather(slot):
        for r in range(tm):
            for k in range(TOP_K):
                row_copy(0, k, r, slot).wait()

    @pl.when(i == 0)
    def _():
        start_gather(dest_ref, 0)

    def block(slot):
        wait_gather(slot)
        start_gather(dest_next_ref, 1 - slot)
        acc = wt_ref[:, 0:1] * bufs[slot][0]
        for k in range(1, TOP_K):
            acc = acc + wt_ref[:, k:k + 1] * bufs[slot][k]
        o_ref[...] = x_ref[...] + gate_ref[0] * acc

        @pl.when(i + 1 >= pl.num_programs(0))
        def _():
            wait_gather(1 - slot)

    for parity in range(2):
        @pl.when(lax.rem(i, 2) == parity)
        def _():
            block(parity)


def _combine(ys, dest_blocks, wt, x_all, t_rows, gate, ridx, tm):
    d = x_all.shape[1]
    n_steps = t_rows // tm
    return pl.pallas_call(
        _combine_kernel,
        grid=(n_steps,),
        in_specs=[
            pl.BlockSpec((1, 1, TOP_K * tm), lambda i: (i, 0, 0), memory_space=pltpu.SMEM),
            pl.BlockSpec((1, 1, TOP_K * tm), lambda i: (jnp.minimum(i + 1, n_steps - 1), 0, 0),
                         memory_space=pltpu.SMEM),
            pl.BlockSpec((tm, TOP_K), lambda i: (i, 0)),
            pl.BlockSpec((tm, d), lambda i: (i, 0)),
            pl.BlockSpec((1, 1, d), lambda i: (ridx(i, tm), 0, 0)),
            pl.BlockSpec(memory_space=pl.ANY),
        ],
        out_specs=pl.BlockSpec((tm, d), lambda i: (i, 0)),
        out_shape=jax.ShapeDtypeStruct((t_rows, d), F32),
        scratch_shapes=[pltpu.VMEM((TOP_K, tm, d), F32), pltpu.VMEM((TOP_K, tm, d), F32),
                        pltpu.SemaphoreType.DMA((2,))],
        compiler_params=_cparams(32),
        name="moe_combine",
    )(dest_blocks, dest_blocks, wt, x_all, gate, ys)


def _slot_tokens_kernel(dest_ref, zeros_ref, o_ref, sem, *, tm):
    i = pl.program_id(0)

    @pl.when(i == 0)
    def _():
        fill = pltpu.make_async_copy(zeros_ref, o_ref, sem)
        fill.start()
        fill.wait()

    def body(r, carry):
        for k in range(TOP_K):
            o_ref[dest_ref[0, 0, k * tm + r]] = i * tm + r
        return carry

    lax.fori_loop(0, tm, body, 0, unroll=8)


def _slot_tokens(dest_blocks, n_blocks, tm):
    n_steps = dest_blocks.shape[0]
    return pl.pallas_call(
        functools.partial(_slot_tokens_kernel, tm=tm),
        grid=(n_steps,),
        in_specs=[
            pl.BlockSpec((1, 1, TOP_K * tm), lambda i: (i, 0, 0), memory_space=pltpu.SMEM),
            pl.BlockSpec(memory_space=pl.ANY),
        ],
        out_specs=pl.BlockSpec(memory_space=pltpu.SMEM),
        out_shape=jax.ShapeDtypeStruct((n_blocks * EXPERT_ROWS,), jnp.int32),
        scratch_shapes=[pltpu.SemaphoreType.DMA],
        name="moe_slot_tokens",
    )(dest_blocks, jnp.zeros((n_blocks * EXPERT_ROWS,), jnp.int32))


def _slot_blocks(dest, tm):
    k, t = dest.shape
    return dest.reshape(k, t // tm, tm).transpose(1, 0, 2).reshape(t // tm, 1, k * tm)


def _moe(x_all, t_rows, norm_g, shift, scale, gate, ridx, w_r, b_r, layer, wg, wl, bg, bl, wd, bd,
         split_next=None, combine_tm=128):
    n_e = w_r.shape[1]
    h, top_idx, top_w = _router(x_all, t_rows, norm_g, shift, scale, w_r.T, b_r.reshape(n_e, 1), ridx)
    top_idx = top_idx[:TOP_K]
    top_w = top_w[:TOP_K]

    n_slots = TOP_K * t_rows
    e_flat = top_idx.reshape(n_slots)
    onehot = (e_flat[:, None] == jnp.arange(n_e, dtype=jnp.int32)[None, :]).astype(jnp.int32)
    csum = jnp.cumsum(onehot, axis=0)
    rank = jnp.sum(onehot * csum, axis=1) - 1
    counts = csum[-1]
    padded = (counts + EXPERT_ROWS - 1) // EXPERT_ROWS * EXPERT_ROWS
    pad_ends = jnp.cumsum(padded)
    pad_starts = pad_ends - padded
    dest = (jnp.sum(onehot * pad_starts[None, :], axis=1) + rank).reshape(TOP_K, t_rows)
    n_blocks = -(-(n_slots + n_e * (EXPERT_ROWS - 1)) // EXPERT_ROWS)
    n_pad = n_blocks * EXPERT_ROWS
    blk_row0 = jnp.arange(n_blocks, dtype=jnp.int32) * EXPERT_ROWS
    blk_e = jnp.minimum(jnp.sum((pad_ends[None, :] <= blk_row0[:, None]).astype(jnp.int32), axis=1), n_e - 1)
    n_used = (pad_ends[-1:] // EXPERT_ROWS).astype(jnp.int32)
    tok_blocks = _slot_tokens(_slot_blocks(dest, 512), n_blocks, 512).reshape(n_blocks, 1, EXPERT_ROWS)
    blk_e = blk_e.astype(jnp.int32)
    blk_i = jnp.arange(n_blocks, dtype=jnp.int32)
    first_of_expert = jnp.logical_or(blk_i == 0, blk_e != jnp.roll(blk_e, 1))
    parity = ((jnp.cumsum(first_of_expert.astype(jnp.int32)) - 1) % 2).astype(jnp.int32)
    next_blk = (pad_ends // EXPERT_ROWS).astype(jnp.int32)[blk_e]
    next_e = jnp.where(next_blk < n_used[0], blk_e[jnp.minimum(next_blk, n_blocks - 1)] + layer * n_e, -1)

    up = _expert_up(h, tok_blocks, blk_e, n_used, wg, wl, bg, bl, n_slots, split_next)
    ys = _expert_down(up[0], blk_e + layer * n_e, n_used, next_e.astype(jnp.int32), parity, wd, bd)
    out = _combine(ys, _slot_blocks(dest, combine_tm), top_w.T, x_all, t_rows, gate, ridx, combine_tm)
    return out, tuple(up[1:])


def _rope_tables(seq, pad_rows):
    rows = seq // GRID_W
    row_pos = jnp.repeat(jnp.arange(rows), GRID_W).astype(F32)
    col_pos = jnp.tile(jnp.arange(GRID_W), rows).astype(F32)
    n_freq = HEAD_DIM // 4
    inv_freq = ROPE_THETA ** (-jnp.arange(n_freq, dtype=F32) / n_freq)
    ang_r = row_pos[:, None] * inv_freq
    ang_c = col_pos[:, None] * inv_freq
    cos = jnp.concatenate([jnp.cos(ang_r)] * 2 + [jnp.cos(ang_c)] * 2, axis=1)
    sin = jnp.concatenate([-jnp.sin(ang_r), jnp.sin(ang_r), -jnp.sin(ang_c), jnp.sin(ang_c)], axis=1)
    cos = jnp.concatenate([cos, jnp.ones((pad_rows, HEAD_DIM), F32)], axis=0)
    sin = jnp.concatenate([sin, jnp.zeros((pad_rows, HEAD_DIM), F32)], axis=0)
    return cos, sin


def kernel(x, c, ctx, c_ctx, w_mod, b_mod, norm_mix_g, norm_ffn_g, gmlp_w_in, gmlp_norm_g, gmlp_w_s, gmlp_b_s,
           gmlp_w_out, diff_w_qkv, diff_q_norm_g, diff_k_norm_g, diff_lambda, diff_subln_g, diff_w_o,
           moe_w_router, moe_b_router, moe_w_gate_up, moe_b_gate_up, moe_w_down, moe_b_down):
    batch, seq, d = x.shape
    ctx_len = ctx.shape[1]
    depth = w_mod.shape[0]
    t_lat = batch * seq
    t_ctx = batch * ctx_len
    t_all = t_lat + t_ctx
    assert batch < MOD_ROWS and depth == 2

    def ridx(i, tm):
        return jnp.minimum((i * tm) // seq, batch)

    c16 = jnp.concatenate([c, c_ctx[None, :], jnp.zeros((MOD_ROWS - batch - 1, d), F32)], axis=0)
    mods = _modulation(c16, w_mod, b_mod).reshape(depth, MOD_ROWS, N_MOD, d)

    def mod(layer, k):
        return mods[layer, :, k, :].reshape(MOD_ROWS, 1, d)

    x_all = jnp.concatenate([x.reshape(t_lat, d), ctx.reshape(t_ctx, d)], axis=0)

    n_e, de2 = moe_w_gate_up.shape[1], moe_w_gate_up.shape[3]
    de = de2 // 2
    w_gu_all = moe_w_gate_up.reshape(depth * n_e, d, de2)
    wg0, wl0 = _split_gate_up(w_gu_all, n_e)
    bg = moe_b_gate_up[:, :, 0::2].reshape(depth, n_e, 1, de)
    bl = moe_b_gate_up[:, :, 1::2].reshape(depth, n_e, 1, de)
    wd = moe_w_down.reshape(depth * n_e, de, d)
    bd = moe_b_down.reshape(depth * n_e, 1, d)

    uv = _gmlp_in(x_all, norm_mix_g[0:1], mod(0, 0), mod(0, 1), gmlp_w_in[0].astype(BF16), ridx)
    x_all = _gmlp_out(uv, gmlp_norm_g[0:1], gmlp_w_s[0].astype(BF16), gmlp_b_s[0].T,
                      gmlp_w_out[0].astype(BF16), x_all, mod(0, 2), ridx)
    x_all, (wg1, wl1) = _moe(x_all, t_all, norm_ffn_g[0:1], mod(0, 3), mod(0, 4), mod(0, 5), ridx,
                             moe_w_router[0], moe_b_router[0], 0, wg0, wl0, bg[0], bl[0], wd, bd,
                             split_next=(w_gu_all, n_e, n_e))

    lam_init = 0.8 - 0.6 * math.exp(-0.3 * 1)
    lf = diff_lambda[0].astype(F32)
    lam = (jnp.exp(jnp.sum(lf[0] * lf[1])) - jnp.exp(jnp.sum(lf[2] * lf[3])) + lam_init).reshape(1)
    tm_qkv = 1024
    cos_t, sin_t = _rope_tables(seq, tm_qkv)
    n_heads2 = d // HEAD_DIM
    q_scale = HEAD_DIM ** -0.5 * math.log2(math.e)
    head_gain = jnp.concatenate([jnp.tile(diff_q_norm_g[0] * q_scale, n_heads2),
                                 jnp.tile(diff_k_norm_g[0], n_heads2),
                                 jnp.ones((d,), F32)]).reshape(1, 3 * d)
    qkv = _qkv(x_all, t_all, t_lat, norm_mix_g[1:2], mod(1, 0), mod(1, 1), diff_w_qkv[0].astype(BF16), head_gain,
               cos_t, sin_t, ridx, seq, tm=tm_qkv)
    attn = _diff_attention(qkv, lam, diff_subln_g[0:1], batch, seq, ctx_len, d, 1.0 - lam_init)
    x_lat = _proj_residual(attn, diff_w_o[0].astype(BF16), x_all, mod(1, 2), ridx)
    x_lat, _ = _moe(x_lat, t_lat, norm_ffn_g[1:2], mod(1, 3), mod(1, 4), mod(1, 5), ridx,
                    moe_w_router[1], moe_b_router[1], 1, wg1, wl1, bg[1], bl[1], wd, bd)
    return x_lat.reshape(batch, seq, d)
```

```python
import functools
import math

import jax
import jax.numpy as jnp
from jax import lax
from jax.experimental import pallas as pl
from jax.experimental.pallas import tpu as pltpu

F32 = jnp.float32
BF16 = jnp.bfloat16

NORM_EPS = 1e-6
N_MOD = 6
GRID_W = 64
GMLP_CHUNK = 128
GMLP_GROUPS = 8
DIFF_HEADS = 8
HEAD_DIM = 128
ROPE_THETA = 10000.0
N_EXPERTS = 32
TOP_K = 4
SWIGLU_ALPHA = 1.702
SWIGLU_LIMIT = 7.0

MOD_ROWS = 16
EXPERT_ROWS = 256
MIB = 1024 * 1024


def _cparams(vmem_mib):
    return pltpu.CompilerParams(vmem_limit_bytes=vmem_mib * MIB)


def _rms_mod(x, g, shift, scale):
    ms = jnp.mean(x * x, axis=-1, keepdims=True)
    y = x * lax.rsqrt(ms + NORM_EPS) * g
    return y * (1.0 + scale) + shift


def _mod_kernel(c_ref, w_ref, b_ref, o_ref):
    c = c_ref[...]
    s = c * jax.nn.sigmoid(c)
    o_ref[0] = jnp.dot(s.astype(BF16), w_ref[0].astype(BF16), preferred_element_type=F32) + b_ref[0]


def _modulation(c16, w_mod, b_mod):
    depth, d, n = w_mod.shape
    tn = 1024
    return pl.pallas_call(
        _mod_kernel,
        grid=(depth, n // tn),
        in_specs=[
            pl.BlockSpec((MOD_ROWS, d), lambda l, j: (0, 0)),
            pl.BlockSpec((1, d, tn), lambda l, j: (l, 0, j)),
            pl.BlockSpec((1, 1, tn), lambda l, j: (l, 0, j)),
        ],
        out_specs=pl.BlockSpec((1, MOD_ROWS, tn), lambda l, j: (l, 0, j)),
        out_shape=jax.ShapeDtypeStruct((depth, MOD_ROWS, n), F32),
        compiler_params=_cparams(40),
        name="modulation",
    )(c16, w_mod, b_mod.reshape(depth, 1, n))


def _gmlp_in_kernel(x_ref, g_ref, sh_ref, sc_ref, w_ref, o_ref, h_ref):
    @pl.when(pl.program_id(1) == 0)
    def _():
        h_ref[...] = _rms_mod(x_ref[...], g_ref[...], sh_ref[0], sc_ref[0]).astype(BF16)

    acc = jnp.dot(h_ref[...], w_ref[...], preferred_element_type=F32)
    gelu = 0.5 * acc * (1.0 + lax.erf(acc * (2.0 ** -0.5)))
    o_ref[...] = gelu.astype(o_ref.dtype)


def _gmlp_in(x, g, shift, scale, w_bf16, ridx, tm=1024, tn=1024):
    t, d = x.shape
    n = w_bf16.shape[1]
    return pl.pallas_call(
        _gmlp_in_kernel,
        grid=(t // tm, n // tn),
        in_specs=[
            pl.BlockSpec((tm, d), lambda i, j: (i, 0)),
            pl.BlockSpec((1, d), lambda i, j: (0, 0)),
            pl.BlockSpec((1, 1, d), lambda i, j: (ridx(i, tm), 0, 0)),
            pl.BlockSpec((1, 1, d), lambda i, j: (ridx(i, tm), 0, 0)),
            pl.BlockSpec((d, tn), lambda i, j: (0, j)),
        ],
        out_specs=pl.BlockSpec((tm, tn), lambda i, j: (i, j)),
        out_shape=jax.ShapeDtypeStruct((t, n), BF16),
        scratch_shapes=[pltpu.VMEM((tm, d), BF16)],
        compiler_params=_cparams(48),
        name="gmlp_in",
    )(x, g, shift, scale, w_bf16)


def _gmlp_out_kernel(u_ref, v_ref, lng_ref, ws_ref, bs_ref, wo_ref, x_ref, gate_ref, o_ref, t_ref):
    tm, width = v_ref.shape
    gw = width // GMLP_GROUPS
    v = v_ref[...].astype(F32)
    mu = jnp.mean(v, axis=-1, keepdims=True)
    vc = v - mu
    var = jnp.mean(vc * vc, axis=-1, keepdims=True)
    vn = (vc * lax.rsqrt(var + NORM_EPS) * lng_ref[...]).astype(BF16)
    for c in range(tm // GMLP_CHUNK):
        r0 = c * GMLP_CHUNK
        for g in range(GMLP_GROUPS):
            c0 = g * gw
            mixed = jnp.dot(ws_ref[g], vn[r0:r0 + GMLP_CHUNK, c0:c0 + gw], preferred_element_type=F32)
            mixed = mixed + bs_ref[:, g:g + 1]
            u = u_ref[r0:r0 + GMLP_CHUNK, c0:c0 + gw].astype(F32)
            t_ref[r0:r0 + GMLP_CHUNK, c0:c0 + gw] = (u * mixed).astype(BF16)
    y = jnp.dot(t_ref[...], wo_ref[...], preferred_element_type=F32)
    o_ref[...] = x_ref[...] + gate_ref[0] * y


def _gmlp_out(uv, ln_g, w_s_bf16, b_s_t, w_out_bf16, x, gate, ridx, tm=512):
    t, d = x.shape
    width = uv.shape[1] // 2
    return pl.pallas_call(
        _gmlp_out_kernel,
        grid=(t // tm,),
        in_specs=[
            pl.BlockSpec((tm, width), lambda i: (i, 0)),
            pl.BlockSpec((tm, width), lambda i: (i, 1)),
            pl.BlockSpec((1, width), lambda i: (0, 0)),
            pl.BlockSpec(w_s_bf16.shape, lambda i: (0, 0, 0)),
            pl.BlockSpec(b_s_t.shape, lambda i: (0, 0)),
            pl.BlockSpec((width, d), lambda i: (0, 0)),
            pl.BlockSpec((tm, d), lambda i: (i, 0)),
            pl.BlockSpec((1, 1, d), lambda i: (ridx(i, tm), 0, 0)),
        ],
        out_specs=pl.BlockSpec((tm, d), lambda i: (i, 0)),
        out_shape=jax.ShapeDtypeStruct((t, d), F32),
        scratch_shapes=[pltpu.VMEM((tm, width), BF16)],
        compiler_params=_cparams(56),
        name="gmlp_out",
    )(uv, uv, ln_g, w_s_bf16, b_s_t, w_out_bf16, x, gate)


def _qkv_kernel(x_ref, g_ref, sh_ref, sc_ref, w_ref, hg_ref, cos_ref, sin_ref, o_ref, h_ref, *, n_qk_tiles):
    j = pl.program_id(1)

    @pl.when(j == 0)
    def _():
        h_ref[...] = _rms_mod(x_ref[...], g_ref[...], sh_ref[0], sc_ref[0]).astype(BF16)

    acc = jnp.dot(h_ref[...], w_ref[...], preferred_element_type=F32)
    tm, tn = acc.shape

    @pl.when(j < n_qk_tiles)
    def _():
        cos = cos_ref[...]
        sin = sin_ref[...]
        first_half = (lax.broadcasted_iota(jnp.int32, (tm, HEAD_DIM), 1) % (HEAD_DIM // 2)) < (HEAD_DIM // 4)
        for hh in range(tn // HEAD_DIM):
            a = acc[:, hh * HEAD_DIM:(hh + 1) * HEAD_DIM]
            ms = jnp.mean(a * a, axis=-1, keepdims=True)
            a = a * lax.rsqrt(ms + NORM_EPS) * hg_ref[:, hh * HEAD_DIM:(hh + 1) * HEAD_DIM]
            partner = jnp.where(first_half,
                                pltpu.roll(a, HEAD_DIM - HEAD_DIM // 4, 1),
                                pltpu.roll(a, HEAD_DIM // 4, 1))
            o_ref[:, hh * HEAD_DIM:(hh + 1) * HEAD_DIM] = (a * cos + partner * sin).astype(o_ref.dtype)

    @pl.when(j >= n_qk_tiles)
    def _():
        o_ref[...] = acc.astype(o_ref.dtype)


def _qkv(x_all, t_rows, t_lat, g, shift, scale, w_bf16, head_gain, cos_t, sin_t, ridx, seq, tm=1024, tn=1024):
    d = x_all.shape[1]
    n = w_bf16.shape[1]
    n_qk_tiles = (2 * n // 3) // tn
    seq_blocks = seq // tm
    n_lat_blocks = t_lat // tm

    def rope_idx(i, j):
        return (jnp.where(i < n_lat_blocks, i % seq_blocks, seq_blocks), 0)

    return pl.pallas_call(
        functools.partial(_qkv_kernel, n_qk_tiles=n_qk_tiles),
        grid=(t_rows // tm, n // tn),
        in_specs=[
            pl.BlockSpec((tm, d), lambda i, j: (i, 0)),
            pl.BlockSpec((1, d), lambda i, j: (0, 0)),
            pl.BlockSpec((1, 1, d), lambda i, j: (ridx(i, tm), 0, 0)),
            pl.BlockSpec((1, 1, d), lambda i, j: (ridx(i, tm), 0, 0)),
            pl.BlockSpec((d, tn), lambda i, j: (0, j)),
            pl.BlockSpec((1, tn), lambda i, j: (0, j)),
            pl.BlockSpec((tm, HEAD_DIM), rope_idx),
            pl.BlockSpec((tm, HEAD_DIM), rope_idx),
        ],
        out_specs=pl.BlockSpec((tm, tn), lambda i, j: (i, j)),
        out_shape=jax.ShapeDtypeStruct((t_rows, n), BF16),
        scratch_shapes=[pltpu.VMEM((tm, d), BF16)],
        compiler_params=_cparams(48),
        name="qkv_proj",
    )(x_all, g, shift, scale, w_bf16, head_gain, cos_t, sin_t)


def _attn_kernel(lam_ref, q1_ref, q2_ref, k1l_ref, k2l_ref, k1c_ref, k2c_ref, vl_ref, vc_ref, sg_ref, o_ref,
                 *, out_scale, key_chunk):
    lam = lam_ref[0]
    nt = (((1,), (1,)), ((), ()))

    def softmax_times_v(q_ref, kc_ref, kl_ref):
        q = q_ref[...]
        tq = q.shape[0]
        chunks = [(kc_ref, vc_ref, c0, min(key_chunk, kc_ref.shape[0] - c0))
                  for c0 in range(0, kc_ref.shape[0], key_chunk)]
        chunks += [(kl_ref, vl_ref, c0, min(key_chunk, kl_ref.shape[0] - c0))
                   for c0 in range(0, kl_ref.shape[0], key_chunk)]
        m = jnp.full((tq, 1), -jnp.inf, F32)
        z = jnp.zeros((tq, 1), F32)
        acc = jnp.zeros((tq, vl_ref.shape[1]), F32)
        for k_ref, v_ref, c0, n in chunks:
            s = lax.dot_general(q, k_ref[c0:c0 + n, :], nt, preferred_element_type=F32)
            m_new = jnp.maximum(m, jnp.max(s, axis=-1, keepdims=True))
            alpha = jnp.exp2(m - m_new)
            p = jnp.exp2(s - m_new)
            z = alpha * z + jnp.sum(p, axis=-1, keepdims=True)
            acc = alpha * acc + jnp.dot(p.astype(BF16), v_ref[c0:c0 + n, :], preferred_element_type=F32)
            m = m_new
        return acc / z

    o = softmax_times_v(q1_ref, k1c_ref, k1l_ref) - lam * softmax_times_v(q2_ref, k2c_ref, k2l_ref)
    ms = jnp.mean(o * o, axis=-1, keepdims=True)
    o = o * lax.rsqrt(ms + NORM_EPS) * sg_ref[...] * out_scale
    o_ref[...] = o.astype(o_ref.dtype)


def _diff_attention(qkv, lam, subln_g, batch, seq, ctx_len, d, out_scale, tq=512, key_chunk=512):
    t_lat = batch * seq
    vd = 2 * HEAD_DIM
    qb = seq // tq
    k_col0 = d // HEAD_DIM
    v_col0 = 2 * d // vd
    ctx_row0 = t_lat // ctx_len

    return pl.pallas_call(
        functools.partial(_attn_kernel, out_scale=out_scale, key_chunk=key_chunk),
        grid=(batch, DIFF_HEADS, qb),
        in_specs=[
            pl.BlockSpec(memory_space=pltpu.SMEM),
            pl.BlockSpec((tq, HEAD_DIM), lambda b, h, q: (b * qb + q, 2 * h)),
            pl.BlockSpec((tq, HEAD_DIM), lambda b, h, q: (b * qb + q, 2 * h + 1)),
            pl.BlockSpec((seq, HEAD_DIM), lambda b, h, q: (b, k_col0 + 2 * h)),
            pl.BlockSpec((seq, HEAD_DIM), lambda b, h, q: (b, k_col0 + 2 * h + 1)),
            pl.BlockSpec((ctx_len, HEAD_DIM), lambda b, h, q: (ctx_row0 + b, k_col0 + 2 * h)),
            pl.BlockSpec((ctx_len, HEAD_DIM), lambda b, h, q: (ctx_row0 + b, k_col0 + 2 * h + 1)),
            pl.BlockSpec((seq, vd), lambda b, h, q: (b, v_col0 + h)),
            pl.BlockSpec((ctx_len, vd), lambda b, h, q: (ctx_row0 + b, v_col0 + h)),
            pl.BlockSpec((1, vd), lambda b, h, q: (0, 0)),
        ],
        out_specs=pl.BlockSpec((tq, vd), lambda b, h, q: (b * qb + q, h)),
        out_shape=jax.ShapeDtypeStruct((t_lat, d), BF16),
        compiler_params=_cparams(56),
        name="diff_attention",
    )(lam, qkv, qkv, qkv, qkv, qkv, qkv, qkv, qkv, subln_g)


def _proj_residual_kernel(a_ref, w_ref, x_ref, gate_ref, o_ref):
    y = jnp.dot(a_ref[...], w_ref[...], preferred_element_type=F32)
    o_ref[...] = x_ref[...] + gate_ref[0] * y


def _proj_residual(a, w_bf16, x_all, gate, ridx, tm=1024, tn=1024):
    t, k = a.shape
    n = w_bf16.shape[1]
    return pl.pallas_call(
        _proj_residual_kernel,
        grid=(t // tm, n // tn),
        in_specs=[
            pl.BlockSpec((tm, k), lambda i, j: (i, 0)),
            pl.BlockSpec((k, tn), lambda i, j: (0, j)),
            pl.BlockSpec((tm, tn), lambda i, j: (i, j)),
            pl.BlockSpec((1, 1, tn), lambda i, j: (ridx(i, tm), 0, j)),
        ],
        out_specs=pl.BlockSpec((tm, tn), lambda i, j: (i, j)),
        out_shape=jax.ShapeDtypeStruct((t, n), F32),
        compiler_params=_cparams(48),
        name="attn_out_proj",
    )(a, w_bf16, x_all, gate)


def _router_kernel(x_ref, g_ref, sh_ref, sc_ref, wr_ref, br_ref, h_ref, idx_ref, wt_ref):
    h = _rms_mod(x_ref[...], g_ref[...], sh_ref[0], sc_ref[0])
    h_ref[...] = h
    logits = lax.dot_general(wr_ref[...], h, (((1,), (1,)), ((), ())),
                             precision=lax.Precision.HIGHEST, preferred_element_type=F32) + br_ref[...]
    n_e, tm = logits.shape
    e_iota = lax.broadcasted_iota(jnp.int32, (n_e, tm), 0).astype(F32)
    vals, idxs = [], []
    for _ in range(TOP_K):
        m = jnp.max(logits, axis=0, keepdims=True)
        idx = jnp.min(jnp.where(logits == m, e_iota, float(n_e)), axis=0, keepdims=True)
        vals.append(m)
        idxs.append(idx)
        logits = jnp.where(e_iota == idx, -jnp.inf, logits)
    exps = [jnp.exp(v - vals[0]) for v in vals]
    denom = exps[0]
    for e in exps[1:]:
        denom = denom + e
    idx_ref[...] = jnp.zeros(idx_ref.shape, jnp.int32)
    wt_ref[...] = jnp.zeros(wt_ref.shape, F32)
    for k in range(TOP_K):
        idx_ref[k:k + 1, :] = idxs[k].astype(jnp.int32)
        wt_ref[k:k + 1, :] = exps[k] / denom


def _router(x_all, t_rows, g, shift, scale, w_r_t, b_r, ridx, tm=512):
    d = x_all.shape[1]
    n_e = w_r_t.shape[0]
    return pl.pallas_call(
        _router_kernel,
        grid=(t_rows // tm,),
        in_specs=[
            pl.BlockSpec((tm, d), lambda i: (i, 0)),
            pl.BlockSpec((1, d), lambda i: (0, 0)),
            pl.BlockSpec((1, 1, d), lambda i: (ridx(i, tm), 0, 0)),
            pl.BlockSpec((1, 1, d), lambda i: (ridx(i, tm), 0, 0)),
            pl.BlockSpec((n_e, d), lambda i: (0, 0)),
            pl.BlockSpec((n_e, 1), lambda i: (0, 0)),
        ],
        out_specs=[
            pl.BlockSpec((tm, d), lambda i: (i, 0)),
            pl.BlockSpec((8, tm), lambda i: (0, i)),
            pl.BlockSpec((8, tm), lambda i: (0, i)),
        ],
        out_shape=[
            jax.ShapeDtypeStruct((t_rows, d), F32),
            jax.ShapeDtypeStruct((8, t_rows), jnp.int32),
            jax.ShapeDtypeStruct((8, t_rows), F32),
        ],
        compiler_params=_cparams(40),
        name="moe_router",
    )(x_all, g, shift, scale, w_r_t, b_r)


PERM_TILE = 256


def _split_gate_up_tile(w_ref, p_ref, og_ref, ol_ref):
    half = PERM_TILE // 2
    tn = w_ref.shape[2]
    for c in range(tn // PERM_TILE):
        w = w_ref[0, :, c * PERM_TILE:(c + 1) * PERM_TILE].astype(BF16)
        t = jnp.dot(w, p_ref[...], preferred_element_type=F32)
        og_ref[0, :, c * half:(c + 1) * half] = t[:, :half].astype(BF16)
        ol_ref[0, :, c * half:(c + 1) * half] = t[:, half:].astype(BF16)


def _split_perm():
    half = PERM_TILE // 2
    col = jnp.arange(PERM_TILE, dtype=jnp.int32)
    src = jnp.where(col < half, 2 * col, 2 * (col - half) + 1)
    return (jnp.arange(PERM_TILE, dtype=jnp.int32)[:, None] == src[None, :]).astype(BF16)


def _split_gate_up(w_gu, n_e, tn=1024):
    _, d, n2 = w_gu.shape
    out = jax.ShapeDtypeStruct((n_e, d, n2 // 2), BF16)
    return pl.pallas_call(
        _split_gate_up_tile,
        grid=(n_e, n2 // tn),
        in_specs=[
            pl.BlockSpec((1, d, tn), lambda e, j: (e, 0, j)),
            pl.BlockSpec((PERM_TILE, PERM_TILE), lambda e, j: (0, 0)),
        ],
        out_specs=[
            pl.BlockSpec((1, d, tn // 2), lambda e, j: (e, 0, j)),
            pl.BlockSpec((1, d, tn // 2), lambda e, j: (e, 0, j)),
        ],
        out_shape=[out, out],
        compiler_params=_cparams(40),
        name="moe_split_gate_up",
    )(w_gu, _split_perm())


SIDE_SPLIT_TILE = 512


GATHER_DEPTH = 3


def _expert_up_kernel(blk_e_ref, n_used_ref, tok_ref, tok_next_ref, tok_next2_ref, h_ref, wg_ref, wl_ref,
                      bg_ref, bl_ref, *rest, nc, side_split):
    if side_split:
        ws_ref, p_ref, o_ref, sg_ref, sl_ref = rest[:5]
        rest = rest[5:]
    else:
        o_ref = rest[0]
        rest = rest[1:]
    xbufs = rest[:GATHER_DEPTH]
    sem = rest[GATHER_DEPTH]
    i = pl.program_id(0)
    n_used = n_used_ref[0]

    def row_copy(src_row, r, slot):
        return pltpu.make_async_copy(h_ref.at[pl.ds(src_row, 1)], xbufs[slot].at[pl.ds(r, 1)], sem.at[slot])

    def start_gather(idx_ref, slot):
        for r in range(EXPERT_ROWS):
            row_copy(idx_ref[0, 0, r], r, slot).start()

    def wait_gather(slot):
        for r in range(EXPERT_ROWS):
            row_copy(0, r, slot).wait()

    @pl.when(i == 0)
    def _():
        start_gather(tok_ref, 0)
        start_gather(tok_next_ref, 1)

    def block(slot):
        wait_gather(slot)
        start_gather(tok_next2_ref, (slot + 2) % GATHER_DEPTH)
        xb = xbufs[slot][...].astype(BF16)
        de = o_ref.shape[1]
        for c in range(de // nc):
            sl = slice(c * nc, (c + 1) * nc)
            gate = jnp.dot(xb, wg_ref[0, :, sl], preferred_element_type=F32) + bg_ref[0, :, sl]
            lin = jnp.dot(xb, wl_ref[0, :, sl], preferred_element_type=F32) + bl_ref[0, :, sl]
            glu = jnp.minimum(gate, SWIGLU_LIMIT)
            lin = jnp.clip(lin, -SWIGLU_LIMIT, SWIGLU_LIMIT)
            act = glu * jax.nn.sigmoid(SWIGLU_ALPHA * glu) * (lin + 1.0)
            o_ref[:, sl] = act.astype(o_ref.dtype)

        if side_split:
            _split_gate_up_tile(ws_ref, p_ref, sg_ref, sl_ref)

        @pl.when(i + 1 >= n_used)
        def _():
            wait_gather((slot + 1) % GATHER_DEPTH)
            wait_gather((slot + 2) % GATHER_DEPTH)

    for ring_slot in range(GATHER_DEPTH):
        @pl.when(jnp.logical_and(i < n_used, lax.rem(i, GATHER_DEPTH) == ring_slot))
        def _():
            block(ring_slot)

    @pl.when(i >= n_used)
    def _():
        o_ref[...] = jnp.zeros(o_ref.shape, o_ref.dtype)


def _expert_up(h, tok_blocks, blk_e, n_used, wg, wl, bg, bl, n_slots, split_next=None, nc=512):
    d = wg.shape[1]
    de = wg.shape[2]
    n_blocks = tok_blocks.shape[0]

    def w_idx(i, be, nu):
        return (be[i], 0, 0)

    in_specs = [
        pl.BlockSpec((1, 1, EXPERT_ROWS), lambda i, be, nu: (i, 0, 0), memory_space=pltpu.SMEM),
        pl.BlockSpec((1, 1, EXPERT_ROWS), lambda i, be, nu: (jnp.minimum(i + 1, n_blocks - 1), 0, 0),
                     memory_space=pltpu.SMEM),
        pl.BlockSpec((1, 1, EXPERT_ROWS), lambda i, be, nu: (jnp.minimum(i + 2, n_blocks - 1), 0, 0),
                     memory_space=pltpu.SMEM),
        pl.BlockSpec(memory_space=pl.ANY),
        pl.BlockSpec((1, d, de), w_idx),
        pl.BlockSpec((1, d, de), w_idx),
        pl.BlockSpec((1, 1, de), w_idx),
        pl.BlockSpec((1, 1, de), w_idx),
    ]
    out_specs = [pl.BlockSpec((EXPERT_ROWS, de), lambda i, be, nu: (i, 0))]
    out_shape = [jax.ShapeDtypeStruct((n_blocks * EXPERT_ROWS, de), BF16)]
    operands = [tok_blocks, tok_blocks, tok_blocks, h, wg, wl, bg, bl]
    if split_next is not None:
        w_raw, e0, n_e = split_next
        tiles_per_e = w_raw.shape[2] // SIDE_SPLIT_TILE
        n_tiles = n_e * tiles_per_e
        assert n_slots // EXPERT_ROWS >= n_tiles

        def tile_idx(i, be, nu, first):
            t = jnp.minimum(i, n_tiles - 1)
            return (first + t // tiles_per_e, 0, t % tiles_per_e)

        in_specs += [
            pl.BlockSpec((1, d, SIDE_SPLIT_TILE), functools.partial(tile_idx, first=e0)),
            pl.BlockSpec((PERM_TILE, PERM_TILE), lambda i, be, nu: (0, 0)),
        ]
        split_out = jax.ShapeDtypeStruct((n_e, d, w_raw.shape[2] // 2), BF16)
        out_specs += [pl.BlockSpec((1, d, SIDE_SPLIT_TILE // 2), functools.partial(tile_idx, first=0))] * 2
        out_shape += [split_out, split_out]
        operands += [w_raw, _split_perm()]

    grid_spec = pltpu.PrefetchScalarGridSpec(
        num_scalar_prefetch=2,
        grid=(n_blocks,),
        in_specs=in_specs,
        out_specs=out_specs,
        scratch_shapes=[pltpu.VMEM((EXPERT_ROWS, d), h.dtype)] * GATHER_DEPTH
        + [pltpu.SemaphoreType.DMA((GATHER_DEPTH,))],
    )
    return pl.pallas_call(
        functools.partial(_expert_up_kernel, nc=nc, side_split=split_next is not None),
        grid_spec=grid_spec,
        out_shape=out_shape,
        compiler_params=_cparams(56),
        name="moe_expert_up",
    )(blk_e, n_used, *operands)


def _expert_down_kernel(blk_e_ref, n_used_ref, next_e_ref, parity_ref, a_ref, wd_ref, bd_ref, o_ref,
                        stage0_ref, stage1_ref, wb_ref, sem):
    i = pl.program_id(0)
    new_expert = jnp.logical_or(i == 0, blk_e_ref[i] != blk_e_ref[jnp.maximum(i - 1, 0)])
    stages = (stage0_ref, stage1_ref)

    def weight_copy(expert, slot):
        return pltpu.make_async_copy(wd_ref.at[expert], stages[slot], sem.at[slot])

    @pl.when(i == 0)
    def _():
        weight_copy(blk_e_ref[0], 0).start()

    for parity in range(2):
        @pl.when(jnp.logical_and(jnp.logical_and(i < n_used_ref[0], new_expert), parity_ref[i] == parity))
        def _():
            nxt = next_e_ref[i]

            @pl.when(nxt >= 0)
            def _():
                weight_copy(nxt, 1 - parity).start()

            weight_copy(0, parity).wait()
            wb_ref[...] = stages[parity][...].astype(BF16)

    @pl.when(i < n_used_ref[0])
    def _():
        o_ref[...] = jnp.dot(a_ref[...], wb_ref[...], preferred_element_type=F32) + bd_ref[0]

    @pl.when(i >= n_used_ref[0])
    def _():
        o_ref[...] = jnp.zeros(o_ref.shape, o_ref.dtype)


def _expert_down(act, blk_e, n_used, next_e, parity, wd, bd):
    n_pad, de = act.shape
    d = wd.shape[2]
    n_blocks = n_pad // EXPERT_ROWS

    grid_spec = pltpu.PrefetchScalarGridSpec(
        num_scalar_prefetch=4,
        grid=(n_blocks,),
        in_specs=[
            pl.BlockSpec((EXPERT_ROWS, de), lambda i, be, nu, ne, pa: (jnp.minimum(i, nu[0] - 1), 0)),
            pl.BlockSpec(memory_space=pl.ANY),
            pl.BlockSpec((1, 1, d), lambda i, be, nu, ne, pa: (be[i], 0, 0)),
        ],
        out_specs=pl.BlockSpec((EXPERT_ROWS, d), lambda i, be, nu, ne, pa: (i, 0)),
        scratch_shapes=[pltpu.VMEM((de, d), wd.dtype), pltpu.VMEM((de, d), wd.dtype), pltpu.VMEM((de, d), BF16),
                        pltpu.SemaphoreType.DMA((2,))],
    )
    return pl.pallas_call(
        _expert_down_kernel,
        grid_spec=grid_spec,
        out_shape=jax.ShapeDtypeStruct((n_pad, d), F32),
        compiler_params=_cparams(56),
        name="moe_expert_down",
    )(blk_e, n_used, next_e, parity, act, wd, bd)


def _combine_kernel(dest_ref, dest_next_ref, dest_next2_ref, wt_ref, x_ref, gate_ref, ys_ref, o_ref, *scratch):
    i = pl.program_id(0)
    tm = x_ref.shape[0]
    bufs = scratch[:GATHER_DEPTH]
    sem = scratch[GATHER_DEPTH]

    def row_copy(src_row, k, r, slot):
        return pltpu.make_async_copy(ys_ref.at[pl.ds(src_row, 1)], bufs[slot].at[k, pl.ds(r, 1)], sem.at[slot])

    def start_gather(idx_ref, slot):
        for r in range(tm):
            for k in range(TOP_K):
                row_copy(idx_ref[0, 0, k * tm + r], k, r, slot).start()

    def wait_gather(slot):
        for r in range(tm):
            for k in range(TOP_K):
                row_copy(0, k, r, slot).wait()

    @pl.when(i == 0)
    def _():
        start_gather(dest_ref, 0)
        start_gather(dest_next_ref, 1)

    def block(slot):
        wait_gather(slot)
        start_gather(dest_next2_ref, (slot + 2) % GATHER_DEPTH)
        acc = wt_ref[:, 0:1] * bufs[slot][0]
        for k in range(1, TOP_K):
            acc = acc + wt_ref[:, k:k + 1] * bufs[slot][k]
        o_ref[...] = x_ref[...] + gate_ref[0] * acc

        @pl.when(i + 1 >= pl.num_programs(0))
        def _():
            wait_gather((slot + 1) % GATHER_DEPTH)
            wait_gather((slot + 2) % GATHER_DEPTH)

    for ring_slot in range(GATHER_DEPTH):
        @pl.when(lax.rem(i, GATHER_DEPTH) == ring_slot)
        def _():
            block(ring_slot)


def _combine(ys, dest_blocks, wt, x_all, t_rows, gate, ridx, tm):
    d = x_all.shape[1]
    n_steps = t_rows // tm
    return pl.pallas_call(
        _combine_kernel,
        grid=(n_steps,),
        in_specs=[
            pl.BlockSpec((1, 1, TOP_K * tm), lambda i: (i, 0, 0), memory_space=pltpu.SMEM),
            pl.BlockSpec((1, 1, TOP_K * tm), lambda i: (jnp.minimum(i + 1, n_steps - 1), 0, 0),
                         memory_space=pltpu.SMEM),
            pl.BlockSpec((1, 1, TOP_K * tm), lambda i: (jnp.minimum(i + 2, n_steps - 1), 0, 0),
                         memory_space=pltpu.SMEM),
            pl.BlockSpec((tm, TOP_K), lambda i: (i, 0)),
            pl.BlockSpec((tm, d), lambda i: (i, 0)),
            pl.BlockSpec((1, 1, d), lambda i: (ridx(i, tm), 0, 0)),
            pl.BlockSpec(memory_space=pl.ANY),
        ],
        out_specs=pl.BlockSpec((tm, d), lambda i: (i, 0)),
        out_shape=jax.ShapeDtypeStruct((t_rows, d), F32),
        scratch_shapes=[pltpu.VMEM((TOP_K, tm, d), F32)] * GATHER_DEPTH + [pltpu.SemaphoreType.DMA((GATHER_DEPTH,))],
        compiler_params=_cparams(32),
        name="moe_combine",
    )(dest_blocks, dest_blocks, dest_blocks, wt, x_all, gate, ys)


def _slot_tokens_kernel(dest_ref, zeros_ref, o_ref, sem, *, tm):
    i = pl.program_id(0)

    @pl.when(i == 0)
    def _():
        fill = pltpu.make_async_copy(zeros_ref, o_ref, sem)
        fill.start()
        fill.wait()

    def body(r, carry):
        for k in range(TOP_K):
            o_ref[dest_ref[0, 0, k * tm + r]] = i * tm + r
        return carry

    lax.fori_loop(0, tm, body, 0, unroll=8)


def _slot_tokens(dest_blocks, n_blocks, tm):
    n_steps = dest_blocks.shape[0]
    return pl.pallas_call(
        functools.partial(_slot_tokens_kernel, tm=tm),
        grid=(n_steps,),
        in_specs=[
            pl.BlockSpec((1, 1, TOP_K * tm), lambda i: (i, 0, 0), memory_space=pltpu.SMEM),
            pl.BlockSpec(memory_space=pl.ANY),
        ],
        out_specs=pl.BlockSpec(memory_space=pltpu.SMEM),
        out_shape=jax.ShapeDtypeStruct((n_blocks * EXPERT_ROWS,), jnp.int32),
        scratch_shapes=[pltpu.SemaphoreType.DMA],
        name="moe_slot_tokens",
    )(dest_blocks, jnp.zeros((n_blocks * EXPERT_ROWS,), jnp.int32))


def _slot_blocks(dest, tm):
    k, t = dest.shape
    return dest.reshape(k, t // tm, tm).transpose(1, 0, 2).reshape(t // tm, 1, k * tm)


def _moe(x_all, t_rows, norm_g, shift, scale, gate, ridx, w_r, b_r, layer, wg, wl, bg, bl, wd, bd,
         split_next=None, combine_tm=128):
    n_e = w_r.shape[1]
    h, top_idx, top_w = _router(x_all, t_rows, norm_g, shift, scale, w_r.T, b_r.reshape(n_e, 1), ridx)
    top_idx = top_idx[:TOP_K]
    top_w = top_w[:TOP_K]

    n_slots = TOP_K * t_rows
    e_flat = top_idx.reshape(n_slots)
    onehot = (e_flat[:, None] == jnp.arange(n_e, dtype=jnp.int32)[None, :]).astype(jnp.int32)
    csum = jnp.cumsum(onehot, axis=0)
    rank = jnp.sum(onehot * csum, axis=1) - 1
    counts = csum[-1]
    padded = (counts + EXPERT_ROWS - 1) // EXPERT_ROWS * EXPERT_ROWS
    pad_ends = jnp.cumsum(padded)
    pad_starts = pad_ends - padded
    dest = (jnp.sum(onehot * pad_starts[None, :], axis=1) + rank).reshape(TOP_K, t_rows)
    n_blocks = -(-(n_slots + n_e * (EXPERT_ROWS - 1)) // EXPERT_ROWS)
    n_pad = n_blocks * EXPERT_ROWS
    blk_row0 = jnp.arange(n_blocks, dtype=jnp.int32) * EXPERT_ROWS
    blk_e = jnp.minimum(jnp.sum((pad_ends[None, :] <= blk_row0[:, None]).astype(jnp.int32), axis=1), n_e - 1)
    n_used = (pad_ends[-1:] // EXPERT_ROWS).astype(jnp.int32)
    tok_blocks = _slot_tokens(_slot_blocks(dest, 512), n_blocks, 512).reshape(n_blocks, 1, EXPERT_ROWS)
    blk_e = blk_e.astype(jnp.int32)
    blk_i = jnp.arange(n_blocks, dtype=jnp.int32)
    first_of_expert = jnp.logical_or(blk_i == 0, blk_e != jnp.roll(blk_e, 1))
    parity = ((jnp.cumsum(first_of_expert.astype(jnp.int32)) - 1) % 2).astype(jnp.int32)
    next_blk = (pad_ends // EXPERT_ROWS).astype(jnp.int32)[blk_e]
    next_e = jnp.where(next_blk < n_used[0], blk_e[jnp.minimum(next_blk, n_blocks - 1)] + layer * n_e, -1)

    up = _expert_up(h, tok_blocks, blk_e, n_used, wg, wl, bg, bl, n_slots, split_next)
    ys = _expert_down(up[0], blk_e + layer * n_e, n_used, next_e.astype(jnp.int32), parity, wd, bd)
    out = _combine(ys, _slot_blocks(dest, combine_tm), top_w.T, x_all, t_rows, gate, ridx, combine_tm)
    return out, tuple(up[1:])


def _rope_tables(seq, pad_rows):
    rows = seq // GRID_W
    row_pos = jnp.repeat(jnp.arange(rows), GRID_W).astype(F32)
    col_pos = jnp.tile(jnp.arange(GRID_W), rows).astype(F32)
    n_freq = HEAD_DIM // 4
    inv_freq = ROPE_THETA ** (-jnp.arange(n_freq, dtype=F32) / n_freq)
    ang_r = row_pos[:, None] * inv_freq
    ang_c = col_pos[:, None] * inv_freq
    cos = jnp.concatenate([jnp.cos(ang_r)] * 2 + [jnp.cos(ang_c)] * 2, axis=1)
    sin = jnp.concatenate([-jnp.sin(ang_r), jnp.sin(ang_r), -jnp.sin(ang_c), jnp.sin(ang_c)], axis=1)
    cos = jnp.concatenate([cos, jnp.ones((pad_rows, HEAD_DIM), F32)], axis=0)
    sin = jnp.concatenate([sin, jnp.zeros((pad_rows, HEAD_DIM), F32)], axis=0)
    return cos, sin


def kernel(x, c, ctx, c_ctx, w_mod, b_mod, norm_mix_g, norm_ffn_g, gmlp_w_in, gmlp_norm_g, gmlp_w_s, gmlp_b_s,
           gmlp_w_out, diff_w_qkv, diff_q_norm_g, diff_k_norm_g, diff_lambda, diff_subln_g, diff_w_o,
           moe_w_router, moe_b_router, moe_w_gate_up, moe_b_gate_up, moe_w_down, moe_b_down):
    batch, seq, d = x.shape
    ctx_len = ctx.shape[1]
    depth = w_mod.shape[0]
    t_lat = batch * seq
    t_ctx = batch * ctx_len
    t_all = t_lat + t_ctx
    assert batch < MOD_ROWS and depth == 2

    def ridx(i, tm):
        return jnp.minimum((i * tm) // seq, batch)

    c16 = jnp.concatenate([c, c_ctx[None, :], jnp.zeros((MOD_ROWS - batch - 1, d), F32)], axis=0)
    mods = _modulation(c16, w_mod, b_mod).reshape(depth, MOD_ROWS, N_MOD, d)

    def mod(layer, k):
        return mods[layer, :, k, :].reshape(MOD_ROWS, 1, d)

    x_all = jnp.concatenate([x.reshape(t_lat, d), ctx.reshape(t_ctx, d)], axis=0)

    n_e, de2 = moe_w_gate_up.shape[1], moe_w_gate_up.shape[3]
    de = de2 // 2
    w_gu_all = moe_w_gate_up.reshape(depth * n_e, d, de2)
    wg0, wl0 = _split_gate_up(w_gu_all, n_e)
    bg = moe_b_gate_up[:, :, 0::2].reshape(depth, n_e, 1, de)
    bl = moe_b_gate_up[:, :, 1::2].reshape(depth, n_e, 1, de)
    wd = moe_w_down.reshape(depth * n_e, de, d)
    bd = moe_b_down.reshape(depth * n_e, 1, d)

    uv = _gmlp_in(x_all, norm_mix_g[0:1], mod(0, 0), mod(0, 1), gmlp_w_in[0].astype(BF16), ridx)
    x_all = _gmlp_out(uv, gmlp_norm_g[0:1], gmlp_w_s[0].astype(BF16), gmlp_b_s[0].T,
                      gmlp_w_out[0].astype(BF16), x_all, mod(0, 2), ridx)
    x_all, (wg1, wl1) = _moe(x_all, t_all, norm_ffn_g[0:1], mod(0, 3), mod(0, 4), mod(0, 5), ridx,
                             moe_w_router[0], moe_b_router[0], 0, wg0, wl0, bg[0], bl[0], wd, bd,
                             split_next=(w_gu_all, n_e, n_e))

    lam_init = 0.8 - 0.6 * math.exp(-0.3 * 1)
    lf = diff_lambda[0].astype(F32)
    lam = (jnp.exp(jnp.sum(lf[0] * lf[1])) - jnp.exp(jnp.sum(lf[2] * lf[3])) + lam_init).reshape(1)
    tm_qkv = 1024
    cos_t, sin_t = _rope_tables(seq, tm_qkv)
    n_heads2 = d // HEAD_DIM
    q_scale = HEAD_DIM ** -0.5 * math.log2(math.e)
    head_gain = jnp.concatenate([jnp.tile(diff_q_norm_g[0] * q_scale, n_heads2),
                                 jnp.tile(diff_k_norm_g[0], n_heads2),
                                 jnp.ones((d,), F32)]).reshape(1, 3 * d)
    qkv = _qkv(x_all, t_all, t_lat, norm_mix_g[1:2], mod(1, 0), mod(1, 1), diff_w_qkv[0].astype(BF16), head_gain,
               cos_t, sin_t, ridx, seq, tm=tm_qkv)
    attn = _diff_attention(qkv, lam, diff_subln_g[0:1], batch, seq, ctx_len, d, 1.0 - lam_init)
    x_lat = _proj_residual(attn, diff_w_o[0].astype(BF16), x_all, mod(1, 2), ridx)
    x_lat, _ = _moe(x_lat, t_lat, norm_ffn_g[1:2], mod(1, 3), mod(1, 4), mod(1, 5), ridx,
                    moe_w_router[1], moe_b_router[1], 1, wg1, wl1, bg[1], bl[1], wd, bd)
    return x_lat.reshape(batch, seq, d)
```

```python
import functools
import math

import jax
import jax.numpy as jnp
from jax import lax
from jax.experimental import pallas as pl
from jax.experimental.pallas import tpu as pltpu

F32 = jnp.float32
BF16 = jnp.bfloat16

NORM_EPS = 1e-6
N_MOD = 6
GRID_W = 64
GMLP_CHUNK = 128
GMLP_GROUPS = 8
DIFF_HEADS = 8
HEAD_DIM = 128
ROPE_THETA = 10000.0
N_EXPERTS = 32
TOP_K = 4
SWIGLU_ALPHA = 1.702
SWIGLU_LIMIT = 7.0

MOD_ROWS = 16
EXPERT_ROWS = 256
MIB = 1024 * 1024


def _cparams(vmem_mib):
    return pltpu.CompilerParams(vmem_limit_bytes=vmem_mib * MIB)


def _rms_mod(x, g, shift, scale):
    ms = jnp.mean(x * x, axis=-1, keepdims=True)
    y = x * lax.rsqrt(ms + NORM_EPS) * g
    return y * (1.0 + scale) + shift


def _mod_kernel(c_ref, w_ref, b_ref, o_ref):
    c = c_ref[...]
    s = c * jax.nn.sigmoid(c)
    o_ref[0] = jnp.dot(s.astype(BF16), w_ref[0].astype(BF16), preferred_element_type=F32) + b_ref[0]


def _modulation(c16, w_mod, b_mod):
    depth, d, n = w_mod.shape
    tn = 1024
    return pl.pallas_call(
        _mod_kernel,
        grid=(depth, n // tn),
        in_specs=[
            pl.BlockSpec((MOD_ROWS, d), lambda l, j: (0, 0)),
            pl.BlockSpec((1, d, tn), lambda l, j: (l, 0, j)),
            pl.BlockSpec((1, 1, tn), lambda l, j: (l, 0, j)),
        ],
        out_specs=pl.BlockSpec((1, MOD_ROWS, tn), lambda l, j: (l, 0, j)),
        out_shape=jax.ShapeDtypeStruct((depth, MOD_ROWS, n), F32),
        compiler_params=_cparams(40),
        name="modulation",
    )(c16, w_mod, b_mod.reshape(depth, 1, n))


def _gmlp_in_kernel(x_ref, g_ref, sh_ref, sc_ref, w_ref, o_ref, h_ref):
    @pl.when(pl.program_id(1) == 0)
    def _():
        h_ref[...] = _rms_mod(x_ref[...], g_ref[...], sh_ref[0], sc_ref[0]).astype(BF16)

    acc = jnp.dot(h_ref[...], w_ref[...], preferred_element_type=F32)
    gelu = 0.5 * acc * (1.0 + lax.erf(acc * (2.0 ** -0.5)))
    o_ref[...] = gelu.astype(o_ref.dtype)


def _gmlp_in(x, g, shift, scale, w_bf16, ridx, tm=1024, tn=1024):
    t, d = x.shape
    n = w_bf16.shape[1]
    return pl.pallas_call(
        _gmlp_in_kernel,
        grid=(t // tm, n // tn),
        in_specs=[
            pl.BlockSpec((tm, d), lambda i, j: (i, 0)),
            pl.BlockSpec((1, d), lambda i, j: (0, 0)),
            pl.BlockSpec((1, 1, d), lambda i, j: (ridx(i, tm), 0, 0)),
            pl.BlockSpec((1, 1, d), lambda i, j: (ridx(i, tm), 0, 0)),
            pl.BlockSpec((d, tn), lambda i, j: (0, j)),
        ],
        out_specs=pl.BlockSpec((tm, tn), lambda i, j: (i, j)),
        out_shape=jax.ShapeDtypeStruct((t, n), BF16),
        scratch_shapes=[pltpu.VMEM((tm, d), BF16)],
        compiler_params=_cparams(48),
        name="gmlp_in",
    )(x, g, shift, scale, w_bf16)


def _gmlp_out_kernel(u_ref, v_ref, lng_ref, ws_ref, bs_ref, wo_ref, x_ref, gate_ref, o_ref, t_ref):
    tm, width = v_ref.shape
    gw = width // GMLP_GROUPS
    v = v_ref[...].astype(F32)
    mu = jnp.mean(v, axis=-1, keepdims=True)
    vc = v - mu
    var = jnp.mean(vc * vc, axis=-1, keepdims=True)
    vn = (vc * lax.rsqrt(var + NORM_EPS) * lng_ref[...]).astype(BF16)
    for c in range(tm // GMLP_CHUNK):
        r0 = c * GMLP_CHUNK
        for g in range(GMLP_GROUPS):
            c0 = g * gw
            mixed = jnp.dot(ws_ref[g], vn[r0:r0 + GMLP_CHUNK, c0:c0 + gw], preferred_element_type=F32)
            mixed = mixed + bs_ref[:, g:g + 1]
            u = u_ref[r0:r0 + GMLP_CHUNK, c0:c0 + gw].astype(F32)
            t_ref[r0:r0 + GMLP_CHUNK, c0:c0 + gw] = (u * mixed).astype(BF16)
    y = jnp.dot(t_ref[...], wo_ref[...], preferred_element_type=F32)
    o_ref[...] = x_ref[...] + gate_ref[0] * y


def _gmlp_out(uv, ln_g, w_s_bf16, b_s_t, w_out_bf16, x, gate, ridx, tm=512):
    t, d = x.shape
    width = uv.shape[1] // 2
    return pl.pallas_call(
        _gmlp_out_kernel,
        grid=(t // tm,),
        in_specs=[
            pl.BlockSpec((tm, width), lambda i: (i, 0)),
            pl.BlockSpec((tm, width), lambda i: (i, 1)),
            pl.BlockSpec((1, width), lambda i: (0, 0)),
            pl.BlockSpec(w_s_bf16.shape, lambda i: (0, 0, 0)),
            pl.BlockSpec(b_s_t.shape, lambda i: (0, 0)),
            pl.BlockSpec((width, d), lambda i: (0, 0)),
            pl.BlockSpec((tm, d), lambda i: (i, 0)),
            pl.BlockSpec((1, 1, d), lambda i: (ridx(i, tm), 0, 0)),
        ],
        out_specs=pl.BlockSpec((tm, d), lambda i: (i, 0)),
        out_shape=jax.ShapeDtypeStruct((t, d), F32),
        scratch_shapes=[pltpu.VMEM((tm, width), BF16)],
        compiler_params=_cparams(56),
        name="gmlp_out",
    )(uv, uv, ln_g, w_s_bf16, b_s_t, w_out_bf16, x, gate)


def _qkv_kernel(x_ref, g_ref, sh_ref, sc_ref, w_ref, hg_ref, cos_ref, sin_ref, o_ref, h_ref, *, n_qk_tiles):
    j = pl.program_id(1)

    @pl.when(j == 0)
    def _():
        h_ref[...] = _rms_mod(x_ref[...], g_ref[...], sh_ref[0], sc_ref[0]).astype(BF16)

    acc = jnp.dot(h_ref[...], w_ref[...], preferred_element_type=F32)
    tm, tn = acc.shape

    @pl.when(j < n_qk_tiles)
    def _():
        cos = cos_ref[...]
        sin = sin_ref[...]
        first_half = (lax.broadcasted_iota(jnp.int32, (tm, HEAD_DIM), 1) % (HEAD_DIM // 2)) < (HEAD_DIM // 4)
        for hh in range(tn // HEAD_DIM):
            a = acc[:, hh * HEAD_DIM:(hh + 1) * HEAD_DIM]
            ms = jnp.mean(a * a, axis=-1, keepdims=True)
            a = a * lax.rsqrt(ms + NORM_EPS) * hg_ref[:, hh * HEAD_DIM:(hh + 1) * HEAD_DIM]
            partner = jnp.where(first_half,
                                pltpu.roll(a, HEAD_DIM - HEAD_DIM // 4, 1),
                                pltpu.roll(a, HEAD_DIM // 4, 1))
            o_ref[:, hh * HEAD_DIM:(hh + 1) * HEAD_DIM] = (a * cos + partner * sin).astype(o_ref.dtype)

    @pl.when(j >= n_qk_tiles)
    def _():
        o_ref[...] = acc.astype(o_ref.dtype)


def _qkv(x_all, t_rows, t_lat, g, shift, scale, w_bf16, head_gain, cos_t, sin_t, ridx, seq, tm=1024, tn=1024):
    d = x_all.shape[1]
    n = w_bf16.shape[1]
    n_qk_tiles = (2 * n // 3) // tn
    seq_blocks = seq // tm
    n_lat_blocks = t_lat // tm

    def rope_idx(i, j):
        return (jnp.where(i < n_lat_blocks, i % seq_blocks, seq_blocks), 0)

    return pl.pallas_call(
        functools.partial(_qkv_kernel, n_qk_tiles=n_qk_tiles),
        grid=(t_rows // tm, n // tn),
        in_specs=[
            pl.BlockSpec((tm, d), lambda i, j: (i, 0)),
            pl.BlockSpec((1, d), lambda i, j: (0, 0)),
            pl.BlockSpec((1, 1, d), lambda i, j: (ridx(i, tm), 0, 0)),
            pl.BlockSpec((1, 1, d), lambda i, j: (ridx(i, tm), 0, 0)),
            pl.BlockSpec((d, tn), lambda i, j: (0, j)),
            pl.BlockSpec((1, tn), lambda i, j: (0, j)),
            pl.BlockSpec((tm, HEAD_DIM), rope_idx),
            pl.BlockSpec((tm, HEAD_DIM), rope_idx),
        ],
        out_specs=pl.BlockSpec((tm, tn), lambda i, j: (i, j)),
        out_shape=jax.ShapeDtypeStruct((t_rows, n), BF16),
        scratch_shapes=[pltpu.VMEM((tm, d), BF16)],
        compiler_params=_cparams(48),
        name="qkv_proj",
    )(x_all, g, shift, scale, w_bf16, head_gain, cos_t, sin_t)


def _attn_kernel(lam_ref, q1_ref, q2_ref, k1l_ref, k2l_ref, k1c_ref, k2c_ref, vl_ref, vc_ref, sg_ref, o_ref,
                 *, out_scale, key_chunk):
    lam = lam_ref[0]
    nt = (((1,), (1,)), ((), ()))

    def softmax_times_v(q_ref, kc_ref, kl_ref):
        q = q_ref[...]
        tq = q.shape[0]
        chunks = [(kc_ref, vc_ref, c0, min(key_chunk, kc_ref.shape[0] - c0))
                  for c0 in range(0, kc_ref.shape[0], key_chunk)]
        chunks += [(kl_ref, vl_ref, c0, min(key_chunk, kl_ref.shape[0] - c0))
                   for c0 in range(0, kl_ref.shape[0], key_chunk)]
        m = jnp.full((tq, 1), -jnp.inf, F32)
        z = jnp.zeros((tq, 1), F32)
        acc = jnp.zeros((tq, vl_ref.shape[1]), F32)
        for k_ref, v_ref, c0, n in chunks:
            s = lax.dot_general(q, k_ref[c0:c0 + n, :], nt, preferred_element_type=F32)
            m_new = jnp.maximum(m, jnp.max(s, axis=-1, keepdims=True))
            alpha = jnp.exp2(m - m_new)
            p = jnp.exp2(s - m_new)
            z = alpha * z + jnp.sum(p, axis=-1, keepdims=True)
            acc = alpha * acc + jnp.dot(p.astype(BF16), v_ref[c0:c0 + n, :], preferred_element_type=F32)
            m = m_new
        return acc / z

    o = softmax_times_v(q1_ref, k1c_ref, k1l_ref) - lam * softmax_times_v(q2_ref, k2c_ref, k2l_ref)
    ms = jnp.mean(o * o, axis=-1, keepdims=True)
    o = o * lax.rsqrt(ms + NORM_EPS) * sg_ref[...] * out_scale
    o_ref[...] = o.astype(o_ref.dtype)


def _diff_attention(qkv, lam, subln_g, batch, seq, ctx_len, d, out_scale, tq=512, key_chunk=512):
    t_lat = batch * seq
    vd = 2 * HEAD_DIM
    qb = seq // tq
    k_col0 = d // HEAD_DIM
    v_col0 = 2 * d // vd
    ctx_row0 = t_lat // ctx_len

    return pl.pallas_call(
        functools.partial(_attn_kernel, out_scale=out_scale, key_chunk=key_chunk),
        grid=(batch, DIFF_HEADS, qb),
        in_specs=[
            pl.BlockSpec(memory_space=pltpu.SMEM),
            pl.BlockSpec((tq, HEAD_DIM), lambda b, h, q: (b * qb + q, 2 * h)),
            pl.BlockSpec((tq, HEAD_DIM), lambda b, h, q: (b * qb + q, 2 * h + 1)),
            pl.BlockSpec((seq, HEAD_DIM), lambda b, h, q: (b, k_col0 + 2 * h)),
            pl.BlockSpec((seq, HEAD_DIM), lambda b, h, q: (b, k_col0 + 2 * h + 1)),
            pl.BlockSpec((ctx_len, HEAD_DIM), lambda b, h, q: (ctx_row0 + b, k_col0 + 2 * h)),
            pl.BlockSpec((ctx_len, HEAD_DIM), lambda b, h, q: (ctx_row0 + b, k_col0 + 2 * h + 1)),
            pl.BlockSpec((seq, vd), lambda b, h, q: (b, v_col0 + h)),
            pl.BlockSpec((ctx_len, vd), lambda b, h, q: (ctx_row0 + b, v_col0 + h)),
            pl.BlockSpec((1, vd), lambda b, h, q: (0, 0)),
        ],
        out_specs=pl.BlockSpec((tq, vd), lambda b, h, q: (b * qb + q, h)),
        out_shape=jax.ShapeDtypeStruct((t_lat, d), BF16),
        compiler_params=_cparams(56),
        name="diff_attention",
    )(lam, qkv, qkv, qkv, qkv, qkv, qkv, qkv, qkv, subln_g)


def _proj_residual_kernel(a_ref, w_ref, x_ref, gate_ref, o_ref):
    y = jnp.dot(a_ref[...], w_ref[...], preferred_element_type=F32)
    o_ref[...] = x_ref[...] + gate_ref[0] * y


def _proj_residual(a, w_bf16, x_all, gate, ridx, tm=1024, tn=1024):
    t, k = a.shape
    n = w_bf16.shape[1]
    return pl.pallas_call(
        _proj_residual_kernel,
        grid=(t // tm, n // tn),
        in_specs=[
            pl.BlockSpec((tm, k), lambda i, j: (i, 0)),
            pl.BlockSpec((k, tn), lambda i, j: (0, j)),
            pl.BlockSpec((tm, tn), lambda i, j: (i, j)),
            pl.BlockSpec((1, 1, tn), lambda i, j: (ridx(i, tm), 0, j)),
        ],
        out_specs=pl.BlockSpec((tm, tn), lambda i, j: (i, j)),
        out_shape=jax.ShapeDtypeStruct((t, n), F32),
        compiler_params=_cparams(48),
        name="attn_out_proj",
    )(a, w_bf16, x_all, gate)


def _router_kernel(x_ref, g_ref, sh_ref, sc_ref, wr_ref, br_ref, h_ref, idx_ref, wt_ref):
    h = _rms_mod(x_ref[...], g_ref[...], sh_ref[0], sc_ref[0])
    h_ref[...] = h
    logits = lax.dot_general(wr_ref[...], h, (((1,), (1,)), ((), ())),
                             precision=lax.Precision.HIGHEST, preferred_element_type=F32) + br_ref[...]
    n_e, tm = logits.shape
    e_iota = lax.broadcasted_iota(jnp.int32, (n_e, tm), 0).astype(F32)
    vals, idxs = [], []
    for _ in range(TOP_K):
        m = jnp.max(logits, axis=0, keepdims=True)
        idx = jnp.min(jnp.where(logits == m, e_iota, float(n_e)), axis=0, keepdims=True)
        vals.append(m)
        idxs.append(idx)
        logits = jnp.where(e_iota == idx, -jnp.inf, logits)
    exps = [jnp.exp(v - vals[0]) for v in vals]
    denom = exps[0]
    for e in exps[1:]:
        denom = denom + e
    idx_ref[...] = jnp.zeros(idx_ref.shape, jnp.int32)
    wt_ref[...] = jnp.zeros(wt_ref.shape, F32)
    for k in range(TOP_K):
        idx_ref[k:k + 1, :] = idxs[k].astype(jnp.int32)
        wt_ref[k:k + 1, :] = exps[k] / denom


def _router(x_all, t_rows, g, shift, scale, w_r_t, b_r, ridx, tm=512):
    d = x_all.shape[1]
    n_e = w_r_t.shape[0]
    return pl.pallas_call(
        _router_kernel,
        grid=(t_rows // tm,),
        in_specs=[
            pl.BlockSpec((tm, d), lambda i: (i, 0)),
            pl.BlockSpec((1, d), lambda i: (0, 0)),
            pl.BlockSpec((1, 1, d), lambda i: (ridx(i, tm), 0, 0)),
            pl.BlockSpec((1, 1, d), lambda i: (ridx(i, tm), 0, 0)),
            pl.BlockSpec((n_e, d), lambda i: (0, 0)),
            pl.BlockSpec((n_e, 1), lambda i: (0, 0)),
        ],
        out_specs=[
            pl.BlockSpec((tm, d), lambda i: (i, 0)),
            pl.BlockSpec((8, tm), lambda i: (0, i)),
            pl.BlockSpec((8, tm), lambda i: (0, i)),
        ],
        out_shape=[
            jax.ShapeDtypeStruct((t_rows, d), F32),
            jax.ShapeDtypeStruct((8, t_rows), jnp.int32),
            jax.ShapeDtypeStruct((8, t_rows), F32),
        ],
        compiler_params=_cparams(40),
        name="moe_router",
    )(x_all, g, shift, scale, w_r_t, b_r)


PERM_TILE = 256


def _split_gate_up_tile(w_ref, p_ref, og_ref, ol_ref):
    half = PERM_TILE // 2
    tn = w_ref.shape[2]
    for c in range(tn // PERM_TILE):
        w = w_ref[0, :, c * PERM_TILE:(c + 1) * PERM_TILE].astype(BF16)
        t = jnp.dot(w, p_ref[...], preferred_element_type=F32)
        og_ref[0, :, c * half:(c + 1) * half] = t[:, :half].astype(BF16)
        ol_ref[0, :, c * half:(c + 1) * half] = t[:, half:].astype(BF16)


def _split_perm():
    half = PERM_TILE // 2
    col = jnp.arange(PERM_TILE, dtype=jnp.int32)
    src = jnp.where(col < half, 2 * col, 2 * (col - half) + 1)
    return (jnp.arange(PERM_TILE, dtype=jnp.int32)[:, None] == src[None, :]).astype(BF16)


def _split_gate_up(w_gu, n_e, tn=1024):
    _, d, n2 = w_gu.shape
    out = jax.ShapeDtypeStruct((n_e, d, n2 // 2), BF16)
    return pl.pallas_call(
        _split_gate_up_tile,
        grid=(n_e, n2 // tn),
        in_specs=[
            pl.BlockSpec((1, d, tn), lambda e, j: (e, 0, j)),
            pl.BlockSpec((PERM_TILE, PERM_TILE), lambda e, j: (0, 0)),
        ],
        out_specs=[
            pl.BlockSpec((1, d, tn // 2), lambda e, j: (e, 0, j)),
            pl.BlockSpec((1, d, tn // 2), lambda e, j: (e, 0, j)),
        ],
        out_shape=[out, out],
        compiler_params=_cparams(40),
        name="moe_split_gate_up",
    )(w_gu, _split_perm())


SIDE_SPLIT_TILE = 512


GATHER_DEPTH = 3


def _expert_up_kernel(blk_e_ref, n_used_ref, tok_ref, tok_next_ref, tok_next2_ref, h_ref, wg_ref, wl_ref,
                      bg_ref, bl_ref, *rest, nc, side_split):
    if side_split:
        ws_ref, p_ref, o_ref, sg_ref, sl_ref = rest[:5]
        rest = rest[5:]
    else:
        o_ref = rest[0]
        rest = rest[1:]
    xbufs = rest[:GATHER_DEPTH]
    sem = rest[GATHER_DEPTH]
    i = pl.program_id(0)
    n_used = n_used_ref[0]

    def row_copy(src_row, r, slot):
        return pltpu.make_async_copy(h_ref.at[pl.ds(src_row, 1)], xbufs[slot].at[pl.ds(r, 1)], sem.at[slot])

    def start_gather(idx_ref, slot):
        for r in range(EXPERT_ROWS):
            row_copy(idx_ref[0, 0, r], r, slot).start()

    def wait_gather(slot):
        for r in range(EXPERT_ROWS):
            row_copy(0, r, slot).wait()

    @pl.when(i == 0)
    def _():
        start_gather(tok_ref, 0)
        start_gather(tok_next_ref, 1)

    def block(slot):
        wait_gather(slot)
        start_gather(tok_next2_ref, (slot + 2) % GATHER_DEPTH)
        xb = xbufs[slot][...].astype(BF16)
        de = o_ref.shape[1]
        for c in range(de // nc):
            sl = slice(c * nc, (c + 1) * nc)
            gate = jnp.dot(xb, wg_ref[0, :, sl], preferred_element_type=F32) + bg_ref[0, :, sl]
            lin = jnp.dot(xb, wl_ref[0, :, sl], preferred_element_type=F32) + bl_ref[0, :, sl]
            glu = jnp.minimum(gate, SWIGLU_LIMIT)
            lin = jnp.clip(lin, -SWIGLU_LIMIT, SWIGLU_LIMIT)
            act = glu * jax.nn.sigmoid(SWIGLU_ALPHA * glu) * (lin + 1.0)
            o_ref[:, sl] = act.astype(o_ref.dtype)

        if side_split:
            _split_gate_up_tile(ws_ref, p_ref, sg_ref, sl_ref)

        @pl.when(i + 1 >= n_used)
        def _():
            wait_gather((slot + 1) % GATHER_DEPTH)
            wait_gather((slot + 2) % GATHER_DEPTH)

    for ring_slot in range(GATHER_DEPTH):
        @pl.when(jnp.logical_and(i < n_used, lax.rem(i, GATHER_DEPTH) == ring_slot))
        def _():
            block(ring_slot)

    @pl.when(i >= n_used)
    def _():
        o_ref[...] = jnp.zeros(o_ref.shape, o_ref.dtype)


def _expert_up(h, tok_blocks, blk_e, n_used, wg, wl, bg, bl, n_slots, split_next=None, nc=512):
    d = wg.shape[1]
    de = wg.shape[2]
    n_blocks = tok_blocks.shape[0]

    def w_idx(i, be, nu):
        return (be[i], 0, 0)

    in_specs = [
        pl.BlockSpec((1, 1, EXPERT_ROWS), lambda i, be, nu: (i, 0, 0), memory_space=pltpu.SMEM),
        pl.BlockSpec((1, 1, EXPERT_ROWS), lambda i, be, nu: (jnp.minimum(i + 1, n_blocks - 1), 0, 0),
                     memory_space=pltpu.SMEM),
        pl.BlockSpec((1, 1, EXPERT_ROWS), lambda i, be, nu: (jnp.minimum(i + 2, n_blocks - 1), 0, 0),
                     memory_space=pltpu.SMEM),
        pl.BlockSpec(memory_space=pl.ANY),
        pl.BlockSpec((1, d, de), w_idx),
        pl.BlockSpec((1, d, de), w_idx),
        pl.BlockSpec((1, 1, de), w_idx),
        pl.BlockSpec((1, 1, de), w_idx),
    ]
    out_specs = [pl.BlockSpec((EXPERT_ROWS, de), lambda i, be, nu: (i, 0))]
    out_shape = [jax.ShapeDtypeStruct((n_blocks * EXPERT_ROWS, de), BF16)]
    operands = [tok_blocks, tok_blocks, tok_blocks, h, wg, wl, bg, bl]
    if split_next is not None:
        w_raw, e0, n_e = split_next
        tiles_per_e = w_raw.shape[2] // SIDE_SPLIT_TILE
        n_tiles = n_e * tiles_per_e
        assert n_slots // EXPERT_ROWS >= n_tiles

        def tile_idx(i, be, nu, first):
            t = jnp.minimum(i, n_tiles - 1)
            return (first + t // tiles_per_e, 0, t % tiles_per_e)

        in_specs += [
            pl.BlockSpec((1, d, SIDE_SPLIT_TILE), functools.partial(tile_idx, first=e0)),
            pl.BlockSpec((PERM_TILE, PERM_TILE), lambda i, be, nu: (0, 0)),
        ]
        split_out = jax.ShapeDtypeStruct((n_e, d, w_raw.shape[2] // 2), BF16)
        out_specs += [pl.BlockSpec((1, d, SIDE_SPLIT_TILE // 2), functools.partial(tile_idx, first=0))] * 2
        out_shape += [split_out, split_out]
        operands += [w_raw, _split_perm()]

    grid_spec = pltpu.PrefetchScalarGridSpec(
        num_scalar_prefetch=2,
        grid=(n_blocks,),
        in_specs=in_specs,
        out_specs=out_specs,
        scratch_shapes=[pltpu.VMEM((EXPERT_ROWS, d), h.dtype)] * GATHER_DEPTH
        + [pltpu.SemaphoreType.DMA((GATHER_DEPTH,))],
    )
    return pl.pallas_call(
        functools.partial(_expert_up_kernel, nc=nc, side_split=split_next is not None),
        grid_spec=grid_spec,
        out_shape=out_shape,
        compiler_params=_cparams(56),
        name="moe_expert_up",
    )(blk_e, n_used, *operands)


def _expert_down_kernel(blk_e_ref, n_used_ref, next_e_ref, parity_ref, a_ref, wd_ref, bd_ref, o_ref,
                        stage0_ref, stage1_ref, wb_ref, sem):
    i = pl.program_id(0)
    new_expert = jnp.logical_or(i == 0, blk_e_ref[i] != blk_e_ref[jnp.maximum(i - 1, 0)])
    stages = (stage0_ref, stage1_ref)

    def weight_copy(expert, slot):
        return pltpu.make_async_copy(wd_ref.at[expert], stages[slot], sem.at[slot])

    @pl.when(i == 0)
    def _():
        weight_copy(blk_e_ref[0], 0).start()

    for parity in range(2):
        @pl.when(jnp.logical_and(jnp.logical_and(i < n_used_ref[0], new_expert), parity_ref[i] == parity))
        def _():
            nxt = next_e_ref[i]

            @pl.when(nxt >= 0)
            def _():
                weight_copy(nxt, 1 - parity).start()

            weight_copy(0, parity).wait()
            wb_ref[...] = stages[parity][...].astype(BF16)

    @pl.when(i < n_used_ref[0])
    def _():
        o_ref[...] = jnp.dot(a_ref[...], wb_ref[...], preferred_element_type=F32) + bd_ref[0]

    @pl.when(i >= n_used_ref[0])
    def _():
        o_ref[...] = jnp.zeros(o_ref.shape, o_ref.dtype)


def _expert_down(act, blk_e, n_used, next_e, parity, wd, bd):
    n_pad, de = act.shape
    d = wd.shape[2]
    n_blocks = n_pad // EXPERT_ROWS

    grid_spec = pltpu.PrefetchScalarGridSpec(
        num_scalar_prefetch=4,
        grid=(n_blocks,),
        in_specs=[
            pl.BlockSpec((EXPERT_ROWS, de), lambda i, be, nu, ne, pa: (jnp.minimum(i, nu[0] - 1), 0)),
            pl.BlockSpec(memory_space=pl.ANY),
            pl.BlockSpec((1, 1, d), lambda i, be, nu, ne, pa: (be[i], 0, 0)),
        ],
        out_specs=pl.BlockSpec((EXPERT_ROWS, d), lambda i, be, nu, ne, pa: (i, 0)),
        scratch_shapes=[pltpu.VMEM((de, d), wd.dtype), pltpu.VMEM((de, d), wd.dtype), pltpu.VMEM((de, d), BF16),
                        pltpu.SemaphoreType.DMA((2,))],
    )
    return pl.pallas_call(
        _expert_down_kernel,
        grid_spec=grid_spec,
        out_shape=jax.ShapeDtypeStruct((n_pad, d), F32),
        compiler_params=_cparams(56),
        name="moe_expert_down",
    )(blk_e, n_used, next_e, parity, act, wd, bd)


def _combine_kernel(dest_ref, dest_next_ref, dest_next2_ref, wt_ref, x_ref, gate_ref, ys_ref, o_ref, *scratch):
    i = pl.program_id(0)
    tm = x_ref.shape[0]
    bufs = scratch[:GATHER_DEPTH]
    sem = scratch[GATHER_DEPTH]

    def row_copy(src_row, k, r, slot):
        return pltpu.make_async_copy(ys_ref.at[pl.ds(src_row, 1)], bufs[slot].at[k, pl.ds(r, 1)], sem.at[slot])

    def start_gather(idx_ref, slot):
        for r in range(tm):
            for k in range(TOP_K):
                row_copy(idx_ref[0, 0, k * tm + r], k, r, slot).start(priority=k % 2)

    def wait_gather(slot):
        for r in range(tm):
            for k in range(TOP_K):
                row_copy(0, k, r, slot).wait()

    @pl.when(i == 0)
    def _():
        start_gather(dest_ref, 0)
        start_gather(dest_next_ref, 1)

    def block(slot):
        wait_gather(slot)
        start_gather(dest_next2_ref, (slot + 2) % GATHER_DEPTH)
        acc = wt_ref[:, 0:1] * bufs[slot][0]
        for k in range(1, TOP_K):
            acc = acc + wt_ref[:, k:k + 1] * bufs[slot][k]
        o_ref[...] = x_ref[...] + gate_ref[0] * acc

        @pl.when(i + 1 >= pl.num_programs(0))
        def _():
            wait_gather((slot + 1) % GATHER_DEPTH)
            wait_gather((slot + 2) % GATHER_DEPTH)

    for ring_slot in range(GATHER_DEPTH):
        @pl.when(lax.rem(i, GATHER_DEPTH) == ring_slot)
        def _():
            block(ring_slot)


def _combine(ys, dest_blocks, wt, x_all, t_rows, gate, ridx, tm):
    d = x_all.shape[1]
    n_steps = t_rows // tm
    return pl.pallas_call(
        _combine_kernel,
        grid=(n_steps,),
        in_specs=[
            pl.BlockSpec((1, 1, TOP_K * tm), lambda i: (i, 0, 0), memory_space=pltpu.SMEM),
            pl.BlockSpec((1, 1, TOP_K * tm), lambda i: (jnp.minimum(i + 1, n_steps - 1), 0, 0),
                         memory_space=pltpu.SMEM),
            pl.BlockSpec((1, 1, TOP_K * tm), lambda i: (jnp.minimum(i + 2, n_steps - 1), 0, 0),
                         memory_space=pltpu.SMEM),
            pl.BlockSpec((tm, TOP_K), lambda i: (i, 0)),
            pl.BlockSpec((tm, d), lambda i: (i, 0)),
            pl.BlockSpec((1, 1, d), lambda i: (ridx(i, tm), 0, 0)),
            pl.BlockSpec(memory_space=pl.ANY),
        ],
        out_specs=pl.BlockSpec((tm, d), lambda i: (i, 0)),
        out_shape=jax.ShapeDtypeStruct((t_rows, d), F32),
        scratch_shapes=[pltpu.VMEM((TOP_K, tm, d), F32)] * GATHER_DEPTH + [pltpu.SemaphoreType.DMA((GATHER_DEPTH,))],
        compiler_params=_cparams(32),
        name="moe_combine",
    )(dest_blocks, dest_blocks, dest_blocks, wt, x_all, gate, ys)


def _slot_tokens_kernel(dest_ref, zeros_ref, o_ref, sem, *, tm):
    i = pl.program_id(0)

    @pl.when(i == 0)
    def _():
        fill = pltpu.make_async_copy(zeros_ref, o_ref, sem)
        fill.start()
        fill.wait()

    def body(r, carry):
        for k in range(TOP_K):
            o_ref[dest_ref[0, 0, k * tm + r]] = i * tm + r
        return carry

    lax.fori_loop(0, tm, body, 0, unroll=8)


def _slot_tokens(dest_blocks, n_blocks, tm):
    n_steps = dest_blocks.shape[0]
    return pl.pallas_call(
        functools.partial(_slot_tokens_kernel, tm=tm),
        grid=(n_steps,),
        in_specs=[
            pl.BlockSpec((1, 1, TOP_K * tm), lambda i: (i, 0, 0), memory_space=pltpu.SMEM),
            pl.BlockSpec(memory_space=pl.ANY),
        ],
        out_specs=pl.BlockSpec(memory_space=pltpu.SMEM),
        out_shape=jax.ShapeDtypeStruct((n_blocks * EXPERT_ROWS,), jnp.int32),
        scratch_shapes=[pltpu.SemaphoreType.DMA],
        name="moe_slot_tokens",
    )(dest_blocks, jnp.zeros((n_blocks * EXPERT_ROWS,), jnp.int32))


def _slot_blocks(dest, tm):
    k, t = dest.shape
    return dest.reshape(k, t // tm, tm).transpose(1, 0, 2).reshape(t // tm, 1, k * tm)


def _moe(x_all, t_rows, norm_g, shift, scale, gate, ridx, w_r, b_r, layer, wg, wl, bg, bl, wd, bd,
         split_next=None, combine_tm=128):
    n_e = w_r.shape[1]
    h, top_idx, top_w = _router(x_all, t_rows, norm_g, shift, scale, w_r.T, b_r.reshape(n_e, 1), ridx)
    top_idx = top_idx[:TOP_K]
    top_w = top_w[:TOP_K]

    n_slots = TOP_K * t_rows
    e_flat = top_idx.reshape(n_slots)
    onehot = (e_flat[:, None] == jnp.arange(n_e, dtype=jnp.int32)[None, :]).astype(jnp.int32)
    csum = jnp.cumsum(onehot, axis=0)
    rank = jnp.sum(onehot * csum, axis=1) - 1
    counts = csum[-1]
    padded = (counts + EXPERT_ROWS - 1) // EXPERT_ROWS * EXPERT_ROWS
    pad_ends = jnp.cumsum(padded)
    pad_starts = pad_ends - padded
    dest = (jnp.sum(onehot * pad_starts[None, :], axis=1) + rank).reshape(TOP_K, t_rows)
    n_blocks = -(-(n_slots + n_e * (EXPERT_ROWS - 1)) // EXPERT_ROWS)
    n_pad = n_blocks * EXPERT_ROWS
    blk_row0 = jnp.arange(n_blocks, dtype=jnp.int32) * EXPERT_ROWS
    blk_e = jnp.minimum(jnp.sum((pad_ends[None, :] <= blk_row0[:, None]).astype(jnp.int32), axis=1), n_e - 1)
    n_used = (pad_ends[-1:] // EXPERT_ROWS).astype(jnp.int32)
    tok_blocks = _slot_tokens(_slot_blocks(dest, 512), n_blocks, 512).reshape(n_blocks, 1, EXPERT_ROWS)
    blk_e = blk_e.astype(jnp.int32)
    blk_i = jnp.arange(n_blocks, dtype=jnp.int32)
    first_of_expert = jnp.logical_or(blk_i == 0, blk_e != jnp.roll(blk_e, 1))
    parity = ((jnp.cumsum(first_of_expert.astype(jnp.int32)) - 1) % 2).astype(jnp.int32)
    next_blk = (pad_ends // EXPERT_ROWS).astype(jnp.int32)[blk_e]
    next_e = jnp.where(next_blk < n_used[0], blk_e[jnp.minimum(next_blk, n_blocks - 1)] + layer * n_e, -1)

    up = _expert_up(h, tok_blocks, blk_e, n_used, wg, wl, bg, bl, n_slots, split_next)
    ys = _expert_down(up[0], blk_e + layer * n_e, n_used, next_e.astype(jnp.int32), parity, wd, bd)
    out = _combine(ys, _slot_blocks(dest, combine_tm), top_w.T, x_all, t_rows, gate, ridx, combine_tm)
    return out, tuple(up[1:])


def _rope_tables(seq, pad_rows):
    rows = seq // GRID_W
    row_pos = jnp.repeat(jnp.arange(rows), GRID_W).astype(F32)
    col_pos = jnp.tile(jnp.arange(GRID_W), rows).astype(F32)
    n_freq = HEAD_DIM // 4
    inv_freq = ROPE_THETA ** (-jnp.arange(n_freq, dtype=F32) / n_freq)
    ang_r = row_pos[:, None] * inv_freq
    ang_c = col_pos[:, None] * inv_freq
    cos = jnp.concatenate([jnp.cos(ang_r)] * 2 + [jnp.cos(ang_c)] * 2, axis=1)
    sin = jnp.concatenate([-jnp.sin(ang_r), jnp.sin(ang_r), -jnp.sin(ang_c), jnp.sin(ang_c)], axis=1)
    cos = jnp.concatenate([cos, jnp.ones((pad_rows, HEAD_DIM), F32)], axis=0)
    sin = jnp.concatenate([sin, jnp.zeros((pad_rows, HEAD_DIM), F32)], axis=0)
    return cos, sin


def kernel(x, c, ctx, c_ctx, w_mod, b_mod, norm_mix_g, norm_ffn_g, gmlp_w_in, gmlp_norm_g, gmlp_w_s, gmlp_b_s,
           gmlp_w_out, diff_w_qkv, diff_q_norm_g, diff_k_norm_g, diff_lambda, diff_subln_g, diff_w_o,
           moe_w_router, moe_b_router, moe_w_gate_up, moe_b_gate_up, moe_w_down, moe_b_down):
    batch, seq, d = x.shape
    ctx_len = ctx.shape[1]
    depth = w_mod.shape[0]
    t_lat = batch * seq
    t_ctx = batch * ctx_len
    t_all = t_lat + t_ctx
    assert batch < MOD_ROWS and depth == 2

    def ridx(i, tm):
        return jnp.minimum((i * tm) // seq, batch)

    c16 = jnp.concatenate([c, c_ctx[None, :], jnp.zeros((MOD_ROWS - batch - 1, d), F32)], axis=0)
    mods = _modulation(c16, w_mod, b_mod).reshape(depth, MOD_ROWS, N_MOD, d)

    def mod(layer, k):
        return mods[layer, :, k, :].reshape(MOD_ROWS, 1, d)

    x_all = jnp.concatenate([x.reshape(t_lat, d), ctx.reshape(t_ctx, d)], axis=0)

    n_e, de2 = moe_w_gate_up.shape[1], moe_w_gate_up.shape[3]
    de = de2 // 2
    w_gu_all = moe_w_gate_up.reshape(depth * n_e, d, de2)
    wg0, wl0 = _split_gate_up(w_gu_all, n_e)
    bg = moe_b_gate_up[:, :, 0::2].reshape(depth, n_e, 1, de)
    bl = moe_b_gate_up[:, :, 1::2].reshape(depth, n_e, 1, de)
    wd = moe_w_down.reshape(depth * n_e, de, d)
    bd = moe_b_down.reshape(depth * n_e, 1, d)

    uv = _gmlp_in(x_all, norm_mix_g[0:1], mod(0, 0), mod(0, 1), gmlp_w_in[0].astype(BF16), ridx)
    x_all = _gmlp_out(uv, gmlp_norm_g[0:1], gmlp_w_s[0].astype(BF16), gmlp_b_s[0].T,
                      gmlp_w_out[0].astype(BF16), x_all, mod(0, 2), ridx)
    x_all, (wg1, wl1) = _moe(x_all, t_all, norm_ffn_g[0:1], mod(0, 3), mod(0, 4), mod(0, 5), ridx,
                             moe_w_router[0], moe_b_router[0], 0, wg0, wl0, bg[0], bl[0], wd, bd,
                             split_next=(w_gu_all, n_e, n_e))

    lam_init = 0.8 - 0.6 * math.exp(-0.3 * 1)
    lf = diff_lambda[0].astype(F32)
    lam = (jnp.exp(jnp.sum(lf[0] * lf[1])) - jnp.exp(jnp.sum(lf[2] * lf[3])) + lam_init).reshape(1)
    tm_qkv = 1024
    cos_t, sin_t = _rope_tables(seq, tm_qkv)
    n_heads2 = d // HEAD_DIM
    q_scale = HEAD_DIM ** -0.5 * math.log2(math.e)
    head_gain = jnp.concatenate([jnp.tile(diff_q_norm_g[0] * q_scale, n_heads2),
                                 jnp.tile(diff_k_norm_g[0], n_heads2),
                                 jnp.ones((d,), F32)]).reshape(1, 3 * d)
    qkv = _qkv(x_all, t_all, t_lat, norm_mix_g[1:2], mod(1, 0), mod(1, 1), diff_w_qkv[0].astype(BF16), head_gain,
               cos_t, sin_t, ridx, seq, tm=tm_qkv)
    attn = _diff_attention(qkv, lam, diff_subln_g[0:1], batch, seq, ctx_len, d, 1.0 - lam_init)
    x_lat = _proj_residual(attn, diff_w_o[0].astype(BF16), x_all, mod(1, 2), ridx)
    x_lat, _ = _moe(x_lat, t_lat, norm_ffn_g[1:2], mod(1, 3), mod(1, 4), mod(1, 5), ridx,
                    moe_w_router[1], moe_b_router[1], 1, wg1, wl1, bg[1], bl[1], wd, bd)
    return x_lat.reshape(batch, seq, d)
```
